```python
import jax, jax.numpy as jnp
from jax import lax
import numpy as np

D_MODEL = 1024
BATCH = 16
SEQ = 2048
DEPTH = 2
DEC_BATCH = 128
DEC_SEQ = 8
PAST_LEN = 16384
PAGE_SIZE = 128

N_AB = (DEPTH + 1) // 2
N_C = DEPTH // 2
RMS_EPS = 1e-6
LN_EPS = 1e-5
RET_HEADS = 4
RET_DK = 128
RET_DV = 128
RET_WIDTH = RET_HEADS * RET_DV
RET_CHUNK = 128
ROPE_BASE = 10000.0
GM_GROUPS = 4
GM_GROUP_DIM = 128
GM_WIDTH = GM_GROUPS * GM_GROUP_DIM
GM_CHUNK = 128
AB_SPLITS = [RET_HEADS * RET_DK, 2 * RET_HEADS * RET_DK, 2 * RET_HEADS * RET_DK + RET_WIDTH,
             2 * RET_HEADS * RET_DK + 2 * RET_WIDTH, 2 * RET_HEADS * RET_DK + 2 * RET_WIDTH + GM_WIDTH]
AB_IN = 2 * RET_HEADS * RET_DK + 2 * RET_WIDTH + 2 * GM_WIDTH
AB_OUT = RET_WIDTH + GM_WIDTH
SWA_HEADS = 16
SWA_KV_HEADS = 2
SWA_HD = 64
SWA_GROUP = SWA_HEADS // SWA_KV_HEADS
WINDOW = 128
SWA_SPLITS = [SWA_HEADS * SWA_HD, (SWA_HEADS + SWA_KV_HEADS) * SWA_HD]
SWA_IN = (SWA_HEADS + 2 * SWA_KV_HEADS) * SWA_HD
SWA_OUT = SWA_HEADS * SWA_HD
D_FF = 4 * D_MODEL

kernel_name = "retnet_gmlp_swa_hybrid_step"


def rmsnorm(x, g):
    x32 = x.astype(jnp.float32)
    y = x32 * lax.rsqrt(jnp.mean(x32 * x32, axis=-1, keepdims=True) + RMS_EPS)
    return (y * g.astype(jnp.float32)).astype(x.dtype)


def rotary(x, pos):
    half = x.shape[-1] // 2
    inv = ROPE_BASE ** (-jnp.arange(half, dtype=jnp.float32) / half)
    ang = pos.astype(jnp.float32)[:, None] * inv[None, :]
    cos = jnp.cos(ang)[None, :, None, :]
    sin = jnp.sin(ang)[None, :, None, :]
    x32 = x.astype(jnp.float32)
    x1, x2 = x32[..., :half], x32[..., half:]
    return jnp.concatenate([x1 * cos - x2 * sin, x2 * cos + x1 * sin], axis=-1)


def retention(q, k, v, s0):
    B, L = q.shape[0], q.shape[1]
    C = RET_CHUNK if L % RET_CHUNK == 0 else L
    NC = L // C
    log_g = jnp.log1p(-jnp.exp2(-5.0 - jnp.arange(RET_HEADS, dtype=jnp.float32)))
    idx = jnp.arange(C, dtype=jnp.float32)
    diff = idx[:, None] - idx[None, :]
    decay_intra = jnp.where(diff[None] >= 0, jnp.exp(log_g[:, None, None] * jnp.maximum(diff, 0.0)[None]), 0.0)
    q_dec = jnp.exp(log_g[None, :] * (idx[:, None] + 1.0))
    k_dec = jnp.exp(log_g[None, :] * (C - 1.0 - idx[:, None]))
    chunk_dec = jnp.exp(log_g * C)
    qc = q.reshape(B, NC, C, RET_HEADS, RET_DK) * (RET_DK ** -0.5)
    kc = k.reshape(B, NC, C, RET_HEADS, RET_DK)
    vc = v.reshape(B, NC, C, RET_HEADS, RET_DV)
    scores = jnp.einsum('bnihd,bnjhd->bnhij', qc, kc) * decay_intra[None, None]
    intra = jnp.einsum('bnhij,bnjhv->bnihv', scores, vc)
    kv_chunk = jnp.einsum('bnjhd,bnjhv->nbhdv', kc * k_dec[None, None, :, :, None], vc)

    def step(s, kv):
        return s * chunk_dec[None, :, None, None] + kv, s

    s_final, s_prev = lax.scan(step, s0, kv_chunk)
    cross = jnp.einsum('bnihd,nbhdv->bnihv', qc * q_dec[None, None, :, :, None], s_prev)
    return (intra + cross).reshape(B, L, RET_HEADS, RET_DV), s_final


def gmlp_spatial(u, vn, w_s, b_s):
    B, L = vn.shape[0], vn.shape[1]
    pad = (-L) % GM_CHUNK
    vp = jnp.pad(vn, ((0, 0), (0, pad), (0, 0), (0, 0)))
    NC = (L + pad) // GM_CHUNK
    vc = vp.reshape(B, NC, GM_CHUNK, GM_GROUPS, GM_GROUP_DIM)
    causal = jnp.tril(jnp.ones((GM_CHUNK, GM_CHUNK), dtype=bool))
    w = jnp.where(causal[None], w_s, 0.0).astype(vn.dtype)
    mixed = jnp.einsum('gij,bnjgd->bnigd', w, vc) + b_s.T.astype(vn.dtype)[None, None, :, :, None]
    mixed = mixed.reshape(B, NC * GM_CHUNK, GM_GROUPS, GM_GROUP_DIM)[:, :L]
    return u * mixed


def mixer_ab(h, pos, s0, w_in, w_s, b_s, ln_g, ln_b, w_o):
    B, L, _ = h.shape
    z = h @ w_in
    q, k, v, g, u, gv = jnp.split(z, AB_SPLITS, axis=-1)
    q = rotary(q.reshape(B, L, RET_HEADS, RET_DK), pos)
    k = rotary(k.reshape(B, L, RET_HEADS, RET_DK), pos)
    v = v.reshape(B, L, RET_HEADS, RET_DV).astype(jnp.float32)
    o, s_new = retention(q, k, v, s0.astype(jnp.float32))
    o = o * lax.rsqrt(jnp.mean(o * o, axis=-1, keepdims=True) + RMS_EPS)
    o_ret = (jax.nn.silu(g.astype(jnp.float32)) * o.reshape(B, L, RET_WIDTH)).astype(h.dtype)
    u = jax.nn.gelu(u, approximate=False).reshape(B, L, GM_GROUPS, GM_GROUP_DIM)
    gv = jax.nn.gelu(gv, approximate=False).reshape(B, L, GM_GROUPS, GM_GROUP_DIM).astype(jnp.float32)
    mu = jnp.mean(gv, axis=-1, keepdims=True)
    var = jnp.mean(jnp.square(gv - mu), axis=-1, keepdims=True)
    gvn = ((gv - mu) * lax.rsqrt(var + LN_EPS) * ln_g.astype(jnp.float32) + ln_b.astype(jnp.float32)).astype(h.dtype)
    o_gm = gmlp_spatial(u, gvn, w_s, b_s).reshape(B, L, GM_WIDTH)
    y = jnp.concatenate([o_ret, o_gm], axis=-1) @ w_o
    return y, s_new.astype(s0.dtype), gvn.reshape(B, L, GM_WIDTH)


def swa_project(h, w_qkv, b_qkv):
    B, L, _ = h.shape
    z = h @ w_qkv + b_qkv
    q, k, v = jnp.split(z, SWA_SPLITS, axis=-1)
    q = q.reshape(B, L, SWA_KV_HEADS, SWA_GROUP, SWA_HD)
    k = k.reshape(B, L, SWA_KV_HEADS, SWA_HD)
    v = v.reshape(B, L, SWA_KV_HEADS, SWA_HD)
    return q, k, v


def sink_softmax(scores, mask, sinks):
    s = jnp.where(mask, scores, -jnp.inf)
    sink = sinks.astype(jnp.float32).reshape(SWA_KV_HEADS, SWA_GROUP)[:, :, None, None]
    m = jnp.maximum(jnp.max(s, axis=-1, keepdims=True), sink)
    p = jnp.exp(s - m)
    return p / (jnp.sum(p, axis=-1, keepdims=True) + jnp.exp(sink - m))


def swa_prompt(h, w_qkv, b_qkv, sinks, w_o, b_o):
    B, S, _ = h.shape
    q, k, v = swa_project(h, w_qkv, b_qkv)
    NB = S // WINDOW
    qb = q.reshape(B, NB, WINDOW, SWA_KV_HEADS, SWA_GROUP, SWA_HD).astype(jnp.float32)
    kb = k.reshape(B, NB, WINDOW, SWA_KV_HEADS, SWA_HD).astype(jnp.float32)
    vb = v.reshape(B, NB, WINDOW, SWA_KV_HEADS, SWA_HD).astype(jnp.float32)
    k_band = jnp.concatenate([jnp.pad(kb, ((0, 0), (1, 0), (0, 0), (0, 0), (0, 0)))[:, :-1], kb], axis=2)
    v_band = jnp.concatenate([jnp.pad(vb, ((0, 0), (1, 0), (0, 0), (0, 0), (0, 0)))[:, :-1], vb], axis=2)
    scores = jnp.einsum('bnikgd,bnjkd->bnkgij', qb, k_band) * (SWA_HD ** -0.5)
    i = jnp.arange(WINDOW)[:, None]
    j = jnp.arange(2 * WINDOW)[None, :]
    rel = i + WINDOW - j
    band = (rel >= 0) & (rel <= WINDOW)
    has_prev = jnp.arange(NB)[:, None, None] > 0
    mask = band[None] & (has_prev | (j >= WINDOW)[None])
    p = sink_softmax(scores, mask[None, :, None, None], sinks)
    o = jnp.einsum('bnkgij,bnjkd->bnikgd', p, v_band).reshape(B, S, SWA_OUT).astype(h.dtype)
    return o @ w_o + b_o, k[:, -WINDOW:], v[:, -WINDOW:]


def swa_sample(h, cache_k, cache_v, w_qkv, b_qkv, sinks, w_o, b_o):
    B, L, _ = h.shape
    W = cache_k.shape[1]
    q, k, v = swa_project(h, w_qkv, b_qkv)
    k_all = jnp.concatenate([cache_k.astype(k.dtype), k], axis=1)
    v_all = jnp.concatenate([cache_v.astype(v.dtype), v], axis=1)
    scores = jnp.einsum('bikgd,bjkd->bkgij', q.astype(jnp.float32), k_all.astype(jnp.float32)) * (SWA_HD ** -0.5)
    rel = W + jnp.arange(L)[:, None] - jnp.arange(W + L)[None, :]
    mask = (rel >= 0) & (rel <= WINDOW)
    p = sink_softmax(scores, mask[None, None, None], sinks)
    o = jnp.einsum('bkgij,bjkd->bikgd', p, v_all.astype(jnp.float32)).reshape(B, L, SWA_OUT).astype(h.dtype)
    return o @ w_o + b_o, k_all[:, -WINDOW:], v_all[:, -WINDOW:]


def sqrelu_mlp(h, w_up, w_down):
    return jnp.square(jax.nn.relu(h @ w_up)) @ w_down


def setup_inputs(seed: int = 0) -> dict:
    key = jax.random.key(seed)
    ks = jax.random.split(key, 24)
    f32 = jnp.float32
    nrm = lambda k, shape, scale: jax.random.normal(k, shape, f32) * scale
    return {
        "x_prompt": nrm(ks[0], (BATCH, SEQ, D_MODEL), 1.0),
        "x_sample": nrm(ks[1], (DEC_BATCH, DEC_SEQ, D_MODEL), 1.0),
        "state_ret": nrm(ks[2], (N_AB, DEC_BATCH, RET_HEADS, RET_DK, RET_DV), 1.0),
        "cache_swa_k": nrm(ks[3], (N_C, DEC_BATCH, WINDOW, SWA_KV_HEADS, SWA_HD), 1.0),
        "cache_swa_v": nrm(ks[4], (N_C, DEC_BATCH, WINDOW, SWA_KV_HEADS, SWA_HD), 1.0),
        "norm_mix": 1.0 + nrm(ks[5], (DEPTH, D_MODEL), 0.05),
        "norm_mlp": 1.0 + nrm(ks[6], (DEPTH, D_MODEL), 0.05),
        "norm_final": 1.0 + nrm(ks[7], (D_MODEL,), 0.05),
        "ab_w_in": nrm(ks[8], (N_AB, D_MODEL, AB_IN), D_MODEL ** -0.5),
        "ab_w_s": nrm(ks[9], (N_AB, GM_GROUPS, GM_CHUNK, GM_CHUNK), GM_CHUNK ** -0.5),
        "ab_b_s": 1.0 + nrm(ks[10], (N_AB, GM_GROUPS, GM_CHUNK), 0.1),
        "ab_ln_g": 1.0 + nrm(ks[11], (N_AB, GM_GROUPS, GM_GROUP_DIM), 0.05),
        "ab_ln_b": nrm(ks[12], (N_AB, GM_GROUPS, GM_GROUP_DIM), 0.05),
        "ab_w_o": nrm(ks[13], (N_AB, AB_OUT, D_MODEL), AB_OUT ** -0.5),
        "swa_w_qkv": nrm(ks[14], (N_C, D_MODEL, SWA_IN), D_MODEL ** -0.5),
        "swa_b_qkv": nrm(ks[15], (N_C, SWA_IN), 0.02),
        "swa_sinks": nrm(ks[16], (N_C, SWA_HEADS), 0.5),
        "swa_w_o": nrm(ks[17], (N_C, SWA_OUT, D_MODEL), SWA_OUT ** -0.5),
        "swa_b_o": nrm(ks[18], (N_C, D_MODEL), 0.02),
        "mlp_w_up": nrm(ks[19], (DEPTH, D_MODEL, D_FF), D_MODEL ** -0.5),
        "mlp_w_down": nrm(ks[20], (DEPTH, D_FF, D_MODEL), D_FF ** -0.5),
    }


def reference(x_prompt, x_sample, state_ret, cache_swa_k, cache_swa_v,
              norm_mix, norm_mlp, norm_final,
              ab_w_in, ab_w_s, ab_b_s, ab_ln_g, ab_ln_b, ab_w_o,
              swa_w_qkv, swa_b_qkv, swa_sinks, swa_w_o, swa_b_o,
              mlp_w_up, mlp_w_down):
    B, S, _ = x_prompt.shape
    L = x_sample.shape[1]
    pos_p = jnp.arange(S, dtype=jnp.int32)
    pos_s = PAST_LEN + jnp.arange(L, dtype=jnp.int32)
    hp, hs = x_prompt, x_sample
    ret_p, ret_s, gm_s, kp, vp, ksm, vsm = [], [], [], [], [], [], []
    for l in range(DEPTH):
        hn_p = rmsnorm(hp, norm_mix[l])
        hn_s = rmsnorm(hs, norm_mix[l])
        if l % 2 == 0:
            a = l // 2
            s_zero = jnp.zeros((B, RET_HEADS, RET_DK, RET_DV), dtype=state_ret.dtype)
            yp, sp, _ = mixer_ab(hn_p, pos_p, s_zero, ab_w_in[a], ab_w_s[a], ab_b_s[a], ab_ln_g[a], ab_ln_b[a], ab_w_o[a])
            ys, ss, gvs = mixer_ab(hn_s, pos_s, state_ret[a], ab_w_in[a], ab_w_s[a], ab_b_s[a], ab_ln_g[a], ab_ln_b[a], ab_w_o[a])
            ret_p.append(sp)
            ret_s.append(ss)
            gm_s.append(gvs)
        else:
            c = l // 2
            yp, kpn, vpn = swa_prompt(hn_p, swa_w_qkv[c], swa_b_qkv[c], swa_sinks[c], swa_w_o[c], swa_b_o[c])
            ys, ksn, vsn = swa_sample(hn_s, cache_swa_k[c], cache_swa_v[c], swa_w_qkv[c], swa_b_qkv[c], swa_sinks[c], swa_w_o[c], swa_b_o[c])
            kp.append(kpn)
            vp.append(vpn)
            ksm.append(ksn)
            vsm.append(vsn)
        hp = hp + yp
        hs = hs + ys
        hp = hp + sqrelu_mlp(rmsnorm(hp, norm_mlp[l]), mlp_w_up[l], mlp_w_down[l])
        hs = hs + sqrelu_mlp(rmsnorm(hs, norm_mlp[l]), mlp_w_up[l], mlp_w_down[l])
    y_prompt = rmsnorm(hp, norm_final)
    y_sample = rmsnorm(hs, norm_final)
    return (y_prompt, y_sample, jnp.stack(ret_p), jnp.stack(ret_s), jnp.stack(gm_s),
            jnp.stack(kp), jnp.stack(vp), jnp.stack(ksm), jnp.stack(vsm))
```

```python
import functools

import numpy as np
import jax
import jax.numpy as jnp
from jax import lax
from jax.experimental import pallas as pl
from jax.experimental.pallas import tpu as pltpu

F32 = jnp.float32
BF16 = jnp.bfloat16

D_MODEL = 1024
PAST_LEN = 16384
RMS_EPS = 1e-6
LN_EPS = 1e-5
RET_HEADS = 4
RET_DK = 128
RET_DV = 128
RET_CHUNK = 128
ROPE_BASE = 10000.0
GM_GROUPS = 4
GM_GROUP_DIM = 128
GM_CHUNK = 128
AB_IN = 3072
AB_OUT = 1024
SWA_HEADS = 16
SWA_KV_HEADS = 2
SWA_HD = 64
WINDOW = 128
SWA_IN = 1280
SWA_OUT = 1024
D_FF = 4096

_Q0, _K0, _V0, _G0, _U0, _GV0 = 0, 512, 1024, 1536, 2048, 2560
_SK0, _SV0 = 1024, 1152

LANES = 128
MIXER_ROWS = 512
MLP_ROWS = 512
FF_CHUNK = 1024
SAMPLE_BB = 16
VMEM_LIMIT = 56 * 1024 * 1024


def _rms(x, g):
    return x * lax.rsqrt(jnp.mean(x * x, axis=-1, keepdims=True) + RMS_EPS) * g


def _gelu(x):
    return 0.5 * x * (1.0 + lax.erf(x * np.float32(np.sqrt(0.5))))


def _silu(x):
    return x / (1.0 + jnp.exp(-x))


def _dot(a, b):
    return jnp.dot(a, b, preferred_element_type=F32)


def _dot_nt(a, b):
    return lax.dot_general(a, b, (((1,), (1,)), ((), ())), preferred_element_type=F32)


def _dot_tn(a, b):
    return lax.dot_general(a, b, (((0,), (0,)), ((), ())), preferred_element_type=F32)


def _const_spec(shape):
    zeros = (0,) * len(shape)
    return pl.BlockSpec(shape, lambda *_: zeros, pipeline_mode=pl.Buffered(1))


def _params(*sem):
    return pltpu.CompilerParams(dimension_semantics=sem, vmem_limit_bytes=VMEM_LIMIT)


def _mlp_kernel(x_ref, g_ref, wup_ref, wdn_ref, gf_ref, o_ref, *, final_norm):
    x = x_ref[...]
    hn = _rms(x, g_ref[...]).astype(BF16)
    acc = jnp.zeros_like(x)
    for c in range(D_FF // FF_CHUNK):
        cols = slice(c * FF_CHUNK, (c + 1) * FF_CHUNK)
        h = _dot(hn, wup_ref[:, cols])
        h = jnp.square(jnp.maximum(h, 0.0)).astype(BF16)
        acc = acc + _dot(h, wdn_ref[cols, :])
    y = x + acc
    if final_norm:
        y = _rms(y, gf_ref[...])
    o_ref[...] = y


def _mlp(x, g, wup, wdn, gf, *, final_norm, name):
    rows = x.shape[0]
    tm = min(MLP_ROWS, rows)
    return pl.pallas_call(
        functools.partial(_mlp_kernel, final_norm=final_norm),
        grid=(rows // tm,),
        in_specs=[
            pl.BlockSpec((tm, D_MODEL), lambda i: (i, 0)),
            _const_spec((1, D_MODEL)),
            _const_spec((D_MODEL, D_FF)),
            _const_spec((D_FF, D_MODEL)),
            _const_spec((1, D_MODEL)),
        ],
        out_specs=pl.BlockSpec((tm, D_MODEL), lambda i: (i, 0)),
        out_shape=jax.ShapeDtypeStruct((rows, D_MODEL), F32),
        compiler_params=_params("arbitrary"),
        name=name,
    )(x, g, wup, wdn, gf)


def _rotary(x, cos2, sin2):
    return x * cos2 + pltpu.roll(x, 64, 1) * sin2


def _retention_tables(chunk):
    log_g = jnp.log1p(-jnp.exp2(-5.0 - jnp.arange(RET_HEADS, dtype=F32)))
    idx = jnp.arange(chunk, dtype=F32)
    diff = idx[:, None] - idx[None, :]
    decay = jnp.where(diff[None] >= 0, jnp.exp(log_g[:, None, None] * jnp.maximum(diff, 0.0)[None]), 0.0)
    q_dec = jnp.exp(log_g[:, None] * (idx[None, :] + 1.0))
    k_dec = jnp.exp(log_g[:, None] * (chunk - 1.0 - idx[None, :]))
    log_g64 = np.log1p(-np.exp2(-5.0 - np.arange(RET_HEADS, dtype=np.float64)))
    chunk_dec = tuple(float(v) for v in np.exp(log_g64 * chunk))
    return decay, q_dec, k_dec, chunk_dec


def _rotary_tables(pos):
    half = RET_DK // 2
    inv = ROPE_BASE ** (-jnp.arange(half, dtype=F32) / half)
    ang = pos.astype(F32)[:, None] * inv[None, :]
    cos, sin = jnp.cos(ang), jnp.sin(ang)
    cos2 = jnp.concatenate([cos, cos], axis=-1)
    sin2 = jnp.concatenate([-sin, sin], axis=-1)
    scale = RET_DK ** -0.5
    return cos2 * scale, sin2 * scale, cos2, sin2


def _gmlp_branch(z_ref, lng_ref, lnb_ref, gi):
    u = _gelu(z_ref[:, _U0 + gi * LANES:_U0 + (gi + 1) * LANES])
    gv = _gelu(z_ref[:, _GV0 + gi * LANES:_GV0 + (gi + 1) * LANES])
    mu = jnp.mean(gv, axis=-1, keepdims=True)
    cen = gv - mu
    var = jnp.mean(cen * cen, axis=-1, keepdims=True)
    gvn = cen * lax.rsqrt(var + LN_EPS) * lng_ref[gi:gi + 1, :] + lnb_ref[gi:gi + 1, :]
    return u, gvn


def _ab_prompt_kernel(x_ref, g_ref, win_ref, wo_ref, cq_ref, sq_ref, ck_ref, sk_ref,
                      dec_ref, qd_ref, kd_ref, ws_ref, bs_ref, lng_ref, lnb_ref,
                      o_ref, st_ref, z_ref, cat_ref, s_ref, *, chunk_dec):
    si = pl.program_id(1)
    nchunk = MIXER_ROWS // RET_CHUNK

    @pl.when(si == 0)
    def _():
        s_ref[...] = jnp.zeros_like(s_ref)

    x = x_ref[...]
    hn = _rms(x, g_ref[...]).astype(BF16)
    z_ref[...] = _dot(hn, win_ref[...])

    cq, sq, ck, sk = cq_ref[...], sq_ref[...], ck_ref[...], sk_ref[...]
    for h in range(RET_HEADS):
        lanes = slice(h * LANES, (h + 1) * LANES)
        q = _rotary(z_ref[:, _Q0 + h * LANES:_Q0 + (h + 1) * LANES], cq, sq)
        k = _rotary(z_ref[:, _K0 + h * LANES:_K0 + (h + 1) * LANES], ck, sk)
        dec, qd, kd = dec_ref[h], qd_ref[h], kd_ref[h]
        for c in range(nchunk):
            rows = slice(c * RET_CHUNK, (c + 1) * RET_CHUNK)
            qc, kc = q[rows], k[rows]
            vc = z_ref[rows, _V0 + h * LANES:_V0 + (h + 1) * LANES].astype(BF16)
            state = s_ref[h]
            scores = _dot_nt(qc.astype(BF16), kc.astype(BF16)) * dec
            lhs = jnp.concatenate([scores.astype(BF16), (qc * qd).astype(BF16)], axis=1)
            rhs = jnp.concatenate([vc, state.astype(BF16)], axis=0)
            o = _dot(lhs, rhs)
            kv = _dot_tn((kc * kd).astype(BF16), vc)
            s_ref[h] = state * chunk_dec[h] + kv
            o = o * lax.rsqrt(jnp.mean(o * o, axis=-1, keepdims=True) + RMS_EPS)
            gate = _silu(z_ref[rows, _G0 + h * LANES:_G0 + (h + 1) * LANES])
            cat_ref[rows, lanes] = (gate * o).astype(BF16)

    ri = lax.broadcasted_iota(jnp.int32, (GM_CHUNK, GM_CHUNK), 0)
    ci = lax.broadcasted_iota(jnp.int32, (GM_CHUNK, GM_CHUNK), 1)
    for gi in range(GM_GROUPS):
        u, gvn = _gmlp_branch(z_ref, lng_ref, lnb_ref, gi)
        w = jnp.where(ci <= ri, ws_ref[gi], 0.0).astype(BF16)
        gvb = gvn.astype(BF16)
        rhs = jnp.concatenate([gvb[c * GM_CHUNK:(c + 1) * GM_CHUNK] for c in range(nchunk)], axis=1)
        mixed = _dot(w, rhs)
        bias = bs_ref[gi]
        for c in range(nchunk):
            rows = slice(c * GM_CHUNK, (c + 1) * GM_CHUNK)
            m = mixed[:, c * GM_CHUNK:(c + 1) * GM_CHUNK] + bias
            cat_ref[rows, 512 + gi * LANES:512 + (gi + 1) * LANES] = (u[rows] * m).astype(BF16)

    o_ref[...] = x + _dot(cat_ref[...], wo_ref[...])

    @pl.when(si == pl.num_programs(1) - 1)
    def _():
        st_ref[...] = s_ref[...]


def _ab_prompt(x, g, w_in, w_o, w_s, b_s, ln_g, ln_b, batch, seq):
    nstep = seq // MIXER_ROWS
    decay, q_dec, k_dec, chunk_dec = _retention_tables(RET_CHUNK)
    cq, sq, ck, sk = _rotary_tables(jnp.arange(seq, dtype=jnp.int32))
    lane_b = lambda t: jnp.broadcast_to(t[:, :, None], t.shape + (LANES,))
    row_spec = pl.BlockSpec((MIXER_ROWS, D_MODEL), lambda b, s: (b * nstep + s, 0))
    tab_spec = pl.BlockSpec((MIXER_ROWS, LANES), lambda b, s: (s, 0))
    cube = (RET_HEADS, RET_CHUNK, LANES)
    out, state = pl.pallas_call(
        functools.partial(_ab_prompt_kernel, chunk_dec=chunk_dec),
        grid=(batch, nstep),
        in_specs=[
            row_spec,
            _const_spec((1, D_MODEL)),
            _const_spec((D_MODEL, AB_IN)),
            _const_spec((AB_OUT, D_MODEL)),
            tab_spec, tab_spec, tab_spec, tab_spec,
            _const_spec(cube), _const_spec(cube), _const_spec(cube),
            _const_spec(cube), _const_spec(cube),
            _const_spec((GM_GROUPS, GM_GROUP_DIM)), _const_spec((GM_GROUPS, GM_GROUP_DIM)),
        ],
        out_specs=[
            row_spec,
            pl.BlockSpec((None, RET_HEADS, RET_DK, RET_DV), lambda b, s: (b, 0, 0, 0)),
        ],
        out_shape=[
            jax.ShapeDtypeStruct((batch * seq, D_MODEL), F32),
            jax.ShapeDtypeStruct((batch, RET_HEADS, RET_DK, RET_DV), F32),
        ],
        scratch_shapes=[
            pltpu.VMEM((MIXER_ROWS, AB_IN), F32),
            pltpu.VMEM((MIXER_ROWS, AB_OUT), BF16),
            pltpu.VMEM((RET_HEADS, RET_DK, RET_DV), F32),
        ],
        compiler_params=_params("arbitrary", "arbitrary"),
        name="ab_prompt",
    )(x, g, w_in, w_o, cq, sq, ck, sk, decay, lane_b(q_dec), lane_b(k_dec),
      w_s, lane_b(b_s), ln_g, ln_b)
    return out, state


def _ab_sample_kernel(x_ref, g_ref, win_ref, wo_ref, cq_ref, sq_ref, ck_ref, sk_ref,
                      dec_ref, qd_ref, kd_ref, ws_ref, bs_ref, lng_ref, lnb_ref, causal_ref, st_ref,
                      o_ref, sto_ref, gvn_ref, z_ref, cat_ref, *, chunk_dec, dec_len):
    rows_n = SAMPLE_BB * dec_len
    x = x_ref[...]
    hn = _rms(x, g_ref[...]).astype(BF16)
    z_ref[...] = _dot(hn, win_ref[...])

    cq, sq, ck, sk = cq_ref[...], sq_ref[...], ck_ref[...], sk_ref[...]
    token = lax.broadcasted_iota(jnp.int32, (rows_n, rows_n), 1)
    for h in range(RET_HEADS):
        lanes = slice(h * LANES, (h + 1) * LANES)
        q = _rotary(z_ref[:, _Q0 + h * LANES:_Q0 + (h + 1) * LANES], cq, sq)
        k = _rotary(z_ref[:, _K0 + h * LANES:_K0 + (h + 1) * LANES], ck, sk)
        v = z_ref[:, _V0 + h * LANES:_V0 + (h + 1) * LANES].astype(BF16)
        scores = _dot_nt(q.astype(BF16), k.astype(BF16)) * dec_ref[h]
        intra = _dot(scores.astype(BF16), v)
        qs = q * qd_ref[h]
        kst = (k * kd_ref[h]).T
        kst_all = jnp.concatenate(
            [jnp.where((token >= b * dec_len) & (token < (b + 1) * dec_len), kst, 0.0).astype(BF16)
             for b in range(SAMPLE_BB)], axis=0)
        kv_all = _dot(kst_all, v)
        cross = []
        for b in range(SAMPLE_BB):
            rows = slice(b * dec_len, (b + 1) * dec_len)
            state = st_ref[b, h]
            cross.append(_dot(qs[rows].astype(BF16), state.astype(BF16)))
            sto_ref[b, h] = state * chunk_dec[h] + kv_all[b * RET_DK:(b + 1) * RET_DK]
        o = intra + jnp.concatenate(cross, axis=0)
        o = o * lax.rsqrt(jnp.mean(o * o, axis=-1, keepdims=True) + RMS_EPS)
        gate = _silu(z_ref[:, _G0 + h * LANES:_G0 + (h + 1) * LANES])
        cat_ref[:, lanes] = (gate * o).astype(BF16)

    causal = causal_ref[...] > 0.0
    for gi in range(GM_GROUPS):
        u, gvn = _gmlp_branch(z_ref, lng_ref, lnb_ref, gi)
        gvn_ref[:, gi * LANES:(gi + 1) * LANES] = gvn
        w = jnp.where(causal, ws_ref[gi], 0.0).astype(BF16)
        mixed = _dot(w, gvn.astype(BF16)) + bs_ref[gi]
        cat_ref[:, 512 + gi * LANES:512 + (gi + 1) * LANES] = (u * mixed).astype(BF16)

    o_ref[...] = x + _dot(cat_ref[...], wo_ref[...])


def _ab_sample(x, g, w_in, w_o, w_s, b_s, ln_g, ln_b, state, dec_batch, dec_len):
    rows_n = SAMPLE_BB * dec_len
    assert rows_n == LANES and dec_batch % SAMPLE_BB == 0
    decay, q_dec, k_dec, chunk_dec = _retention_tables(dec_len)
    cq, sq, ck, sk = (jnp.tile(t, (SAMPLE_BB, 1)) for t in
                      _rotary_tables(PAST_LEN + jnp.arange(dec_len, dtype=jnp.int32)))
    eye = jnp.eye(SAMPLE_BB, dtype=F32)
    bd_decay = jax.vmap(lambda d: jnp.kron(eye, d))(decay)
    rows_b = lambda t: jnp.broadcast_to(jnp.tile(t, (1, SAMPLE_BB))[:, :, None], (t.shape[0], rows_n, LANES))
    w_tiled = jnp.tile(w_s[:, :dec_len, :dec_len], (1, SAMPLE_BB, SAMPLE_BB))
    causal = jnp.asarray(np.kron(np.eye(SAMPLE_BB), np.tril(np.ones((dec_len, dec_len)))), F32)
    row_spec = pl.BlockSpec((rows_n, D_MODEL), lambda i: (i, 0))
    st_spec = pl.BlockSpec((SAMPLE_BB, RET_HEADS, RET_DK, RET_DV), lambda i: (i, 0, 0, 0))
    cube = (RET_HEADS, rows_n, LANES)
    tab = (rows_n, LANES)
    return pl.pallas_call(
        functools.partial(_ab_sample_kernel, chunk_dec=chunk_dec, dec_len=dec_len),
        grid=(dec_batch // SAMPLE_BB,),
        in_specs=[
            row_spec,
            _const_spec((1, D_MODEL)),
            _const_spec((D_MODEL, AB_IN)),
            _const_spec((AB_OUT, D_MODEL)),
            _const_spec(tab), _const_spec(tab), _const_spec(tab), _const_spec(tab),
            _const_spec(cube), _const_spec(cube), _const_spec(cube),
            _const_spec(cube), _const_spec(cube),
            _const_spec((GM_GROUPS, GM_GROUP_DIM)), _const_spec((GM_GROUPS, GM_GROUP_DIM)),
            _const_spec(tab),
            st_spec,
        ],
        out_specs=[
            row_spec,
            st_spec,
            pl.BlockSpec((rows_n, GM_GROUPS * GM_GROUP_DIM), lambda i: (i, 0)),
        ],
        out_shape=[
            jax.ShapeDtypeStruct((dec_batch * dec_len, D_MODEL), F32),
            jax.ShapeDtypeStruct(state.shape, F32),
            jax.ShapeDtypeStruct((dec_batch * dec_len, GM_GROUPS * GM_GROUP_DIM), F32),
        ],
        scratch_shapes=[
            pltpu.VMEM((rows_n, AB_IN), F32),
            pltpu.VMEM((rows_n, AB_OUT), BF16),
        ],
        compiler_params=_params("arbitrary"),
        name="ab_sample",
    )(x, g, w_in, w_o, cq, sq, ck, sk, bd_decay, rows_b(q_dec), rows_b(k_dec),
      w_tiled, rows_b(b_s[:, :dec_len]), ln_g, ln_b, causal, state)


def _attend(lhs, ktop, kbot, vtop, vbot, visible, sink_even, sink_odd):
    s = _dot_nt(lhs, jnp.concatenate([ktop, kbot], axis=0))
    s = jnp.where(visible, s, -jnp.inf)
    s0, s1, s2, s3 = (s[:, i * LANES:(i + 1) * LANES] for i in range(4))
    m_even = jnp.max(jnp.maximum(jnp.maximum(s0, s1), sink_even), axis=-1, keepdims=True)
    m_odd = jnp.max(jnp.maximum(jnp.maximum(s2, s3), sink_odd), axis=-1, keepdims=True)
    p = jnp.concatenate([jnp.exp(s0 - m_even), jnp.exp(s1 - m_even),
                         jnp.exp(s2 - m_odd), jnp.exp(s3 - m_odd)], axis=1)
    out = _dot(p.astype(BF16), jnp.concatenate([vtop, vbot], axis=0))
    even_lane = lax.broadcasted_iota(jnp.int32, sink_even.shape, 1) < SWA_HD
    sink_term = jnp.where(even_lane, jnp.exp(sink_even - m_even), jnp.exp(sink_odd - m_odd))
    return out[:, :LANES] / (out[:, LANES:] + sink_term)


def _kv_layouts(kcol, vcol):
    even_lane = lax.broadcasted_iota(jnp.int32, kcol.shape, 1) < SWA_HD
    ks = kcol * (SWA_HD ** -0.5)
    kr = pltpu.roll(ks, SWA_HD, 1)
    vr = pltpu.roll(vcol, SWA_HD, 1)
    one_even = jnp.where(even_lane, 1.0, 0.0)
    one_odd = 1.0 - one_even
    zero = jnp.zeros_like(kcol)
    out = []
    for kvh in range(SWA_KV_HEADS):
        k_lo, k_hi = (ks, kr) if kvh == 0 else (kr, ks)
        v_lo, v_hi = (vcol, vr) if kvh == 0 else (vr, vcol)
        ktop = jnp.where(even_lane, k_lo, zero).astype(BF16)
        kbot = jnp.where(even_lane, zero, k_hi).astype(BF16)
        vtop = jnp.concatenate([jnp.where(even_lane, v_lo, zero), one_even], axis=1).astype(BF16)
        vbot = jnp.concatenate([jnp.where(even_lane, zero, v_hi), one_odd], axis=1).astype(BF16)
        out.append((ktop, kbot, vtop, vbot))
    return out


def _swa_prompt_kernel(x_ref, xh_ref, g_ref, wqkv_ref, bqkv_ref, wo_ref, bo_ref, sink_ref,
                       o_ref, kc_ref, vc_ref, z_ref, cat_ref):
    si = pl.program_id(1)
    nblk = MIXER_ROWS // WINDOW
    npair = SWA_HEADS // SWA_KV_HEADS // 2

    x = x_ref[...]
    hn = _rms(x, g_ref[...]).astype(BF16)
    z_ref[...] = _dot(hn, wqkv_ref[...]) + bqkv_ref[...]
    hh = _rms(xh_ref[...], g_ref[...]).astype(BF16)
    zh = _dot(hh, wqkv_ref[:, _SK0:]) + bqkv_ref[:, _SK0:]

    kcol = z_ref[:, _SK0:_SK0 + LANES]
    vcol = z_ref[:, _SV0:_SV0 + LANES]
    kc_ref[...] = kcol[MIXER_ROWS - WINDOW:]
    vc_ref[...] = vcol[MIXER_ROWS - WINDOW:]
    layouts = _kv_layouts(jnp.concatenate([zh[:, :LANES], kcol], axis=0),
                          jnp.concatenate([zh[:, LANES:], vcol], axis=0))

    rows_n = npair * WINDOW
    qi = lax.broadcasted_iota(jnp.int32, (rows_n, 4 * WINDOW), 0) % WINDOW
    kj = lax.broadcasted_iota(jnp.int32, (rows_n, 4 * WINDOW), 1) % (2 * WINDOW)
    rel = qi + WINDOW - kj
    band = (rel >= 0) & (rel <= WINDOW)
    band_first = band & (kj >= jnp.where(si > 0, 0, WINDOW))
    for r in range(nblk):
        rows = slice(r * WINDOW, (r + 1) * WINDOW)
        keys = slice(r * WINDOW, (r + 2) * WINDOW)
        visible = band_first if r == 0 else band
        for kvh, (ktop, kbot, vtop, vbot) in enumerate(layouts):
            c0 = kvh * npair * LANES
            lhs = jnp.concatenate(
                [z_ref[rows, c0 + p * LANES:c0 + (p + 1) * LANES] for p in range(npair)], axis=0).astype(BF16)
            o = _attend(lhs, ktop[keys], kbot[keys], vtop[keys], vbot[keys], visible,
                        sink_ref[kvh, 0], sink_ref[kvh, 1])
            for p in range(npair):
                cat_ref[rows, c0 + p * LANES:c0 + (p + 1) * LANES] = o[p * WINDOW:(p + 1) * WINDOW].astype(BF16)

    o_ref[...] = x + _dot(cat_ref[...], wo_ref[...]) + bo_ref[...]


def _sink_table(sinks, rows_per_pair):
    npair = SWA_HEADS // SWA_KV_HEADS // 2
    s = sinks.astype(F32).reshape(SWA_KV_HEADS, npair, 2).transpose(0, 2, 1)[:, :, :, None, None]
    s = jnp.broadcast_to(s, (SWA_KV_HEADS, 2, npair, rows_per_pair, LANES))
    return s.reshape(SWA_KV_HEADS, 2, npair * rows_per_pair, LANES)


def _swa_prompt(x, g, w_qkv, b_qkv, sinks, w_o, b_o, batch, seq):
    nstep = seq // MIXER_ROWS
    npair = SWA_HEADS // SWA_KV_HEADS // 2
    row_spec = pl.BlockSpec((MIXER_ROWS, D_MODEL), lambda b, s: (b * nstep + s, 0))
    nblk = MIXER_ROWS // WINDOW
    halo_spec = pl.BlockSpec((WINDOW, D_MODEL), lambda b, s: (jnp.maximum((b * nstep + s) * nblk - 1, 0), 0))
    cache_spec = pl.BlockSpec((None, WINDOW, LANES), lambda b, s: (b, 0, 0))
    return pl.pallas_call(
        _swa_prompt_kernel,
        grid=(batch, nstep),
        in_specs=[
            row_spec,
            halo_spec,
            _const_spec((1, D_MODEL)),
            _const_spec((D_MODEL, SWA_IN)),
            _const_spec((1, SWA_IN)),
            _const_spec((SWA_OUT, D_MODEL)),
            _const_spec((1, D_MODEL)),
            _const_spec((SWA_KV_HEADS, 2, npair * WINDOW, LANES)),
        ],
        out_specs=[row_spec, cache_spec, cache_spec],
        out_shape=[
            jax.ShapeDtypeStruct((batch * seq, D_MODEL), F32),
            jax.ShapeDtypeStruct((batch, WINDOW, LANES), F32),
            jax.ShapeDtypeStruct((batch, WINDOW, LANES), F32),
        ],
        scratch_shapes=[
            pltpu.VMEM((MIXER_ROWS, SWA_IN), F32),
            pltpu.VMEM((MIXER_ROWS, SWA_OUT), BF16),
        ],
        compiler_params=_params("arbitrary", "arbitrary"),
        name="swa_prompt",
    )(x, x, g, w_qkv, b_qkv, w_o, b_o, _sink_table(sinks, WINDOW))


def _swa_sample_kernel(x_ref, g_ref, wqkv_ref, bqkv_ref, wo_ref, bo_ref, sink_ref, ck_ref, cv_ref,
                       o_ref, nk_ref, nv_ref, z_ref, cat_ref, *, dec_len):
    npair = SWA_HEADS // SWA_KV_HEADS // 2
    x = x_ref[...]
    hn = _rms(x, g_ref[...]).astype(BF16)
    z_ref[...] = _dot(hn, wqkv_ref[...]) + bqkv_ref[...]

    rows_n = npair * dec_len
    qi = lax.broadcasted_iota(jnp.int32, (rows_n, 4 * WINDOW), 0) % dec_len
    kj = lax.broadcasted_iota(jnp.int32, (rows_n, 4 * WINDOW), 1) % (2 * WINDOW)
    rel = WINDOW + qi - kj
    visible = (rel >= 0) & (rel <= WINDOW)
    pad = jnp.zeros((WINDOW - dec_len, LANES), F32)
    for b in range(SAMPLE_BB):
        rows = slice(b * dec_len, (b + 1) * dec_len)
        knew = z_ref[rows, _SK0:_SK0 + LANES]
        vnew = z_ref[rows, _SV0:_SV0 + LANES]
        kold, vold = ck_ref[b], cv_ref[b]
        nk_ref[b, 0:WINDOW - dec_len, :] = kold[dec_len:]
        nk_ref[b, WINDOW - dec_len:, :] = knew
        nv_ref[b, 0:WINDOW - dec_len, :] = vold[dec_len:]
        nv_ref[b, WINDOW - dec_len:, :] = vnew
        kall = jnp.concatenate([kold, knew, pad], axis=0)
        vall = jnp.concatenate([vold, vnew, pad], axis=0)
        for kvh, (ktop, kbot, vtop, vbot) in enumerate(_kv_layouts(kall, vall)):
            c0 = kvh * npair * LANES
            lhs = jnp.concatenate(
                [z_ref[rows, c0 + p * LANES:c0 + (p + 1) * LANES] for p in range(npair)], axis=0).astype(BF16)
            o = _attend(lhs, ktop, kbot, vtop, vbot, visible, sink_ref[kvh, 0], sink_ref[kvh, 1])
            for p in range(npair):
                cat_ref[rows, c0 + p * LANES:c0 + (p + 1) * LANES] = o[p * dec_len:(p + 1) * dec_len]

    o_ref[...] = x + _dot(cat_ref[...].astype(BF16), wo_ref[...]) + bo_ref[...]


def _swa_sample(x, g, w_qkv, b_qkv, sinks, w_o, b_o, cache_k, cache_v, dec_batch, dec_len):
    rows_n = SAMPLE_BB * dec_len
    npair = SWA_HEADS // SWA_KV_HEADS // 2
    row_spec = pl.BlockSpec((rows_n, D_MODEL), lambda i: (i, 0))
    cache_spec = pl.BlockSpec((SAMPLE_BB, WINDOW, LANES), lambda i: (i, 0, 0))
    return pl.pallas_call(
        functools.partial(_swa_sample_kernel, dec_len=dec_len),
        grid=(dec_batch // SAMPLE_BB,),
        in_specs=[
            row_spec,
            _const_spec((1, D_MODEL)),
            _const_spec((D_MODEL, SWA_IN)),
            _const_spec((1, SWA_IN)),
            _const_spec((SWA_OUT, D_MODEL)),
            _const_spec((1, D_MODEL)),
            _const_spec((SWA_KV_HEADS, 2, npair * dec_len, LANES)),
            cache_spec, cache_spec,
        ],
        out_specs=[row_spec, cache_spec, cache_spec],
        out_shape=[
            jax.ShapeDtypeStruct((dec_batch * dec_len, D_MODEL), F32),
            jax.ShapeDtypeStruct((dec_batch, WINDOW, LANES), F32),
            jax.ShapeDtypeStruct((dec_batch, WINDOW, LANES), F32),
        ],
        scratch_shapes=[
            pltpu.VMEM((rows_n, SWA_IN), F32),
            pltpu.VMEM((rows_n, SWA_OUT), F32),
        ],
        compiler_params=_params("arbitrary"),
        name="swa_sample",
    )(x, g, w_qkv, b_qkv, w_o, b_o, _sink_table(sinks, dec_len), cache_k, cache_v)


def kernel(x_prompt, x_sample, state_ret, cache_swa_k, cache_swa_v, norm_mix, norm_mlp, norm_final, ab_w_in, ab_w_s, ab_b_s, ab_ln_g, ab_ln_b, ab_w_o, swa_w_qkv, swa_b_qkv, swa_sinks, swa_w_o, swa_b_o, mlp_w_up, mlp_w_down):
    batch, seq, _ = x_prompt.shape
    dec_batch, dec_len, _ = x_sample.shape
    assert seq % MIXER_ROWS == 0 and seq % RET_CHUNK == 0
    kv_lanes = SWA_KV_HEADS * SWA_HD

    hp = x_prompt.reshape(batch * seq, D_MODEL)
    hs = x_sample.reshape(dec_batch * dec_len, D_MODEL)
    row = lambda v: v.reshape(1, -1)
    gfin = row(norm_final)

    w_in, w_o = ab_w_in[0].astype(BF16), ab_w_o[0].astype(BF16)
    hp, ret_p = _ab_prompt(hp, row(norm_mix[0]), w_in, w_o, ab_w_s[0], ab_b_s[0], ab_ln_g[0], ab_ln_b[0],
                           batch, seq)
    hs, ret_s, gm_s = _ab_sample(hs, row(norm_mix[0]), w_in, w_o, ab_w_s[0], ab_b_s[0], ab_ln_g[0], ab_ln_b[0],
                                 state_ret[0], dec_batch, dec_len)
    w_up, w_dn = mlp_w_up[0].astype(BF16), mlp_w_down[0].astype(BF16)
    hp = _mlp(hp, row(norm_mlp[0]), w_up, w_dn, gfin, final_norm=False, name="mlp0_prompt")
    hs = _mlp(hs, row(norm_mlp[0]), w_up, w_dn, gfin, final_norm=False, name="mlp0_sample")

    w_qkv, w_so = swa_w_qkv[0].astype(BF16), swa_w_o[0].astype(BF16)
    hp, kp, vp = _swa_prompt(hp, row(norm_mix[1]), w_qkv, row(swa_b_qkv[0]), swa_sinks[0], w_so, row(swa_b_o[0]),
                             batch, seq)
    hs, ks, vs = _swa_sample(hs, row(norm_mix[1]), w_qkv, row(swa_b_qkv[0]), swa_sinks[0], w_so, row(swa_b_o[0]),
                             cache_swa_k[0].reshape(dec_batch, WINDOW, kv_lanes),
                             cache_swa_v[0].reshape(dec_batch, WINDOW, kv_lanes), dec_batch, dec_len)
    w_up, w_dn = mlp_w_up[1].astype(BF16), mlp_w_down[1].astype(BF16)
    hp = _mlp(hp, row(norm_mlp[1]), w_up, w_dn, gfin, final_norm=True, name="mlp1_prompt")
    hs = _mlp(hs, row(norm_mlp[1]), w_up, w_dn, gfin, final_norm=True, name="mlp1_sample")

    cache_shape = lambda n: (1, n, WINDOW, SWA_KV_HEADS, SWA_HD)
    return (hp.reshape(batch, seq, D_MODEL),
            hs.reshape(dec_batch, dec_len, D_MODEL),
            ret_p[None],
            ret_s[None],
            gm_s.reshape(1, dec_batch, dec_len, GM_GROUPS * GM_GROUP_DIM),
            kp.reshape(cache_shape(batch)),
            vp.reshape(cache_shape(batch)),
            ks.reshape(cache_shape(dec_batch)),
            vs.reshape(cache_shape(dec_batch)))
```

```python
import functools

import numpy as np
import jax
import jax.numpy as jnp
from jax import lax
from jax.experimental import pallas as pl
from jax.experimental.pallas import tpu as pltpu

F32 = jnp.float32
BF16 = jnp.bfloat16

D_MODEL = 1024
PAST_LEN = 16384
RMS_EPS = 1e-6
LN_EPS = 1e-5
RET_HEADS = 4
RET_DK = 128
RET_DV = 128
RET_CHUNK = 128
ROPE_BASE = 10000.0
GM_GROUPS = 4
GM_GROUP_DIM = 128
GM_CHUNK = 128
AB_IN = 3072
AB_OUT = 1024
SWA_HEADS = 16
SWA_KV_HEADS = 2
SWA_HD = 64
WINDOW = 128
SWA_IN = 1280
SWA_OUT = 1024
D_FF = 4096

_Q0, _K0, _V0, _G0, _U0, _GV0 = 0, 512, 1024, 1536, 2048, 2560
_SK0, _SV0 = 1024, 1152

LANES = 128
MIXER_ROWS = 512
MLP_ROWS = 1024
MLP_SPLIT = 2
FF_CHUNK = 1024
SAMPLE_BB = 16
VMEM_LIMIT = 56 * 1024 * 1024


def _rms(x, g):
    return x * lax.rsqrt(jnp.mean(x * x, axis=-1, keepdims=True) + RMS_EPS) * g


def _gelu(x):
    return 0.5 * x * (1.0 + lax.erf(x * np.float32(np.sqrt(0.5))))


def _silu(x):
    return x / (1.0 + jnp.exp(-x))


def _dot(a, b):
    return jnp.dot(a, b, preferred_element_type=F32)


def _dot_nt(a, b):
    return lax.dot_general(a, b, (((1,), (1,)), ((), ())), preferred_element_type=F32)


def _dot_tn(a, b):
    return lax.dot_general(a, b, (((0,), (0,)), ((), ())), preferred_element_type=F32)


def _const_spec(shape):
    zeros = (0,) * len(shape)
    return pl.BlockSpec(shape, lambda *_: zeros, pipeline_mode=pl.Buffered(1))


def _params(*sem):
    return pltpu.CompilerParams(dimension_semantics=sem, vmem_limit_bytes=VMEM_LIMIT)


def _mlp_kernel(x_ref, g_ref, wup_ref, wdn_ref, gf_ref, o_ref, *, final_norm):
    rows = x_ref.shape[0] // MLP_SPLIT
    groups = [slice(i * rows, (i + 1) * rows) for i in range(MLP_SPLIT)]
    xs = [x_ref[r, :] for r in groups]
    hns = [_rms(x, g_ref[...]).astype(BF16) for x in xs]
    accs = [jnp.zeros_like(x) for x in xs]
    for c in range(D_FF // FF_CHUNK):
        cols = slice(c * FF_CHUNK, (c + 1) * FF_CHUNK)
        for i in range(MLP_SPLIT):
            h = _dot(hns[i], wup_ref[:, cols])
            h = jnp.square(jnp.maximum(h, 0.0)).astype(BF16)
            accs[i] = accs[i] + _dot(h, wdn_ref[cols, :])
    for i, r in enumerate(groups):
        y = xs[i] + accs[i]
        if final_norm:
            y = _rms(y, gf_ref[...])
        o_ref[r, :] = y


def _mlp(x, g, wup, wdn, gf, *, final_norm, name):
    rows = x.shape[0]
    tm = min(MLP_ROWS, rows)
    return pl.pallas_call(
        functools.partial(_mlp_kernel, final_norm=final_norm),
        grid=(rows // tm,),
        in_specs=[
            pl.BlockSpec((tm, D_MODEL), lambda i: (i, 0)),
            _const_spec((1, D_MODEL)),
            _const_spec((D_MODEL, D_FF)),
            _const_spec((D_FF, D_MODEL)),
            _const_spec((1, D_MODEL)),
        ],
        out_specs=pl.BlockSpec((tm, D_MODEL), lambda i: (i, 0)),
        out_shape=jax.ShapeDtypeStruct((rows, D_MODEL), F32),
        compiler_params=_params("arbitrary"),
        name=name,
    )(x, g, wup, wdn, gf)


def _rotary(x, cos2, sin2):
    return x * cos2 + pltpu.roll(x, 64, 1) * sin2


def _retention_tables(chunk):
    log_g = jnp.log1p(-jnp.exp2(-5.0 - jnp.arange(RET_HEADS, dtype=F32)))
    idx = jnp.arange(chunk, dtype=F32)
    diff = idx[:, None] - idx[None, :]
    decay = jnp.where(diff[None] >= 0, jnp.exp(log_g[:, None, None] * jnp.maximum(diff, 0.0)[None]), 0.0)
    q_dec = jnp.exp(log_g[:, None] * (idx[None, :] + 1.0))
    k_dec = jnp.exp(log_g[:, None] * (chunk - 1.0 - idx[None, :]))
    log_g64 = np.log1p(-np.exp2(-5.0 - np.arange(RET_HEADS, dtype=np.float64)))
    chunk_dec = tuple(float(v) for v in np.exp(log_g64 * chunk))
    return decay, q_dec, k_dec, chunk_dec


def _rotary_tables(pos):
    half = RET_DK // 2
    inv = ROPE_BASE ** (-jnp.arange(half, dtype=F32) / half)
    ang = pos.astype(F32)[:, None] * inv[None, :]
    cos, sin = jnp.cos(ang), jnp.sin(ang)
    cos2 = jnp.concatenate([cos, cos], axis=-1)
    sin2 = jnp.concatenate([-sin, sin], axis=-1)
    scale = RET_DK ** -0.5
    return cos2 * scale, sin2 * scale, cos2, sin2


def _gmlp_branch(z_ref, lng_ref, lnb_ref, gi):
    u = _gelu(z_ref[:, _U0 + gi * LANES:_U0 + (gi + 1) * LANES])
    gv = _gelu(z_ref[:, _GV0 + gi * LANES:_GV0 + (gi + 1) * LANES])
    mu = jnp.mean(gv, axis=-1, keepdims=True)
    cen = gv - mu
    var = jnp.mean(cen * cen, axis=-1, keepdims=True)
    gvn = cen * lax.rsqrt(var + LN_EPS) * lng_ref[gi:gi + 1, :] + lnb_ref[gi:gi + 1, :]
    return u, gvn


def _ab_prompt_kernel(x_ref, g_ref, win_ref, wo_ref, cq_ref, sq_ref, ck_ref, sk_ref,
                      dec_ref, qd_ref, kd_ref, ws_ref, bs_ref, lng_ref, lnb_ref,
                      o_ref, st_ref, z_ref, cat_ref, s_ref, *, chunk_dec):
    si = pl.program_id(1)
    nchunk = MIXER_ROWS // RET_CHUNK

    @pl.when(si == 0)
    def _():
        s_ref[...] = jnp.zeros_like(s_ref)

    x = x_ref[...]
    hn = _rms(x, g_ref[...]).astype(BF16)
    z_ref[:, :_U0] = _dot(hn, win_ref[:, :_U0])
    z_ref[:, _U0:] = _dot(hn, win_ref[:, _U0:])

    cq, sq, ck, sk = cq_ref[...], sq_ref[...], ck_ref[...], sk_ref[...]
    chunks = [slice(c * RET_CHUNK, (c + 1) * RET_CHUNK) for c in range(nchunk)]
    tile_rows = lambda t: jnp.concatenate([t] * nchunk, axis=0)
    stage1 = []
    for h in range(RET_HEADS):
        q = _rotary(z_ref[:, _Q0 + h * LANES:_Q0 + (h + 1) * LANES], cq, sq)
        k = _rotary(z_ref[:, _K0 + h * LANES:_K0 + (h + 1) * LANES], ck, sk)
        v = z_ref[:, _V0 + h * LANES:_V0 + (h + 1) * LANES].astype(BF16)
        qb, kb = q.astype(BF16), k.astype(BF16)
        qdb = (q * tile_rows(qd_ref[h])).astype(BF16)
        kdb = (k * tile_rows(kd_ref[h])).astype(BF16)
        scores = [_dot_nt(qb[r], kb[r]) for r in chunks]
        kvs = [_dot_tn(kdb[r], v[r]) for r in chunks]
        state = s_ref[h]
        states = []
        for kv in kvs:
            states.append(state.astype(BF16))
            state = state * chunk_dec[h] + kv
        s_ref[h] = state
        stage1.append((v, qdb, scores, states))
    for h, (v, qdb, scores, states) in enumerate(stage1):
        lanes = slice(h * LANES, (h + 1) * LANES)
        dec = dec_ref[h]
        for c, r in enumerate(chunks):
            lhs = jnp.concatenate([(scores[c] * dec).astype(BF16), qdb[r]], axis=1)
            rhs = jnp.concatenate([v[r], states[c]], axis=0)
            o = _dot(lhs, rhs)
            o = o * lax.rsqrt(jnp.mean(o * o, axis=-1, keepdims=True) + RMS_EPS)
            gate = _silu(z_ref[r, _G0 + h * LANES:_G0 + (h + 1) * LANES])
            cat_ref[r, lanes] = (gate * o).astype(BF16)

    ri = lax.broadcasted_iota(jnp.int32, (GM_CHUNK, GM_CHUNK), 0)
    ci = lax.broadcasted_iota(jnp.int32, (GM_CHUNK, GM_CHUNK), 1)
    for gi in range(GM_GROUPS):
        u, gvn = _gmlp_branch(z_ref, lng_ref, lnb_ref, gi)
        w = jnp.where(ci <= ri, ws_ref[gi], 0.0).astype(BF16)
        gvb = gvn.astype(BF16)
        rhs = jnp.concatenate([gvb[c * GM_CHUNK:(c + 1) * GM_CHUNK] for c in range(nchunk)], axis=1)
        mixed = _dot(w, rhs)
        bias = bs_ref[gi]
        for c in range(nchunk):
            rows = slice(c * GM_CHUNK, (c + 1) * GM_CHUNK)
            m = mixed[:, c * GM_CHUNK:(c + 1) * GM_CHUNK] + bias
            cat_ref[rows, 512 + gi * LANES:512 + (gi + 1) * LANES] = (u[rows] * m).astype(BF16)

    o_ref[...] = x + _dot(cat_ref[...], wo_ref[...])

    @pl.when(si == pl.num_programs(1) - 1)
    def _():
        st_ref[...] = s_ref[...]


def _ab_prompt(x, g, w_in, w_o, w_s, b_s, ln_g, ln_b, batch, seq):
    nstep = seq // MIXER_ROWS
    decay, q_dec, k_dec, chunk_dec = _retention_tables(RET_CHUNK)
    cq, sq, ck, sk = _rotary_tables(jnp.arange(seq, dtype=jnp.int32))
    lane_b = lambda t: jnp.broadcast_to(t[:, :, None], t.shape + (LANES,))
    row_spec = pl.BlockSpec((MIXER_ROWS, D_MODEL), lambda b, s: (b * nstep + s, 0))
    tab_spec = pl.BlockSpec((MIXER_ROWS, LANES), lambda b, s: (s, 0))
    cube = (RET_HEADS, RET_CHUNK, LANES)
    out, state = pl.pallas_call(
        functools.partial(_ab_prompt_kernel, chunk_dec=chunk_dec),
        grid=(batch, nstep),
        in_specs=[
            row_spec,
            _const_spec((1, D_MODEL)),
            _const_spec((D_MODEL, AB_IN)),
            _const_spec((AB_OUT, D_MODEL)),
            tab_spec, tab_spec, tab_spec, tab_spec,
            _const_spec(cube), _const_spec(cube), _const_spec(cube),
            _const_spec(cube), _const_spec(cube),
            _const_spec((GM_GROUPS, GM_GROUP_DIM)), _const_spec((GM_GROUPS, GM_GROUP_DIM)),
        ],
        out_specs=[
            row_spec,
            pl.BlockSpec((None, RET_HEADS, RET_DK, RET_DV), lambda b, s: (b, 0, 0, 0)),
        ],
        out_shape=[
            jax.ShapeDtypeStruct((batch * seq, D_MODEL), F32),
            jax.ShapeDtypeStruct((batch, RET_HEADS, RET_DK, RET_DV), F32),
        ],
        scratch_shapes=[
            pltpu.VMEM((MIXER_ROWS, AB_IN), F32),
            pltpu.VMEM((MIXER_ROWS, AB_OUT), BF16),
            pltpu.VMEM((RET_HEADS, RET_DK, RET_DV), F32),
        ],
        compiler_params=_params("arbitrary", "arbitrary"),
        name="ab_prompt",
    )(x, g, w_in, w_o, cq, sq, ck, sk, decay, lane_b(q_dec), lane_b(k_dec),
      w_s, lane_b(b_s), ln_g, ln_b)
    return out, state


def _ab_sample_kernel(x_ref, g_ref, win_ref, wo_ref, cq_ref, sq_ref, ck_ref, sk_ref,
                      dec_ref, qd_ref, kd_ref, ws_ref, bs_ref, lng_ref, lnb_ref, causal_ref, st_ref,
                      o_ref, sto_ref, gvn_ref, z_ref, cat_ref, *, chunk_dec, dec_len):
    rows_n = SAMPLE_BB * dec_len
    x = x_ref[...]
    hn = _rms(x, g_ref[...]).astype(BF16)
    z_ref[...] = _dot(hn, win_ref[...])

    cq, sq, ck, sk = cq_ref[...], sq_ref[...], ck_ref[...], sk_ref[...]
    token = lax.broadcasted_iota(jnp.int32, (rows_n, rows_n), 1)
    for h in range(RET_HEADS):
        lanes = slice(h * LANES, (h + 1) * LANES)
        q = _rotary(z_ref[:, _Q0 + h * LANES:_Q0 + (h + 1) * LANES], cq, sq)
        k = _rotary(z_ref[:, _K0 + h * LANES:_K0 + (h + 1) * LANES], ck, sk)
        v = z_ref[:, _V0 + h * LANES:_V0 + (h + 1) * LANES].astype(BF16)
        scores = _dot_nt(q.astype(BF16), k.astype(BF16)) * dec_ref[h]
        intra = _dot(scores.astype(BF16), v)
        qs = q * qd_ref[h]
        kst = (k * kd_ref[h]).T
        kst_all = jnp.concatenate(
            [jnp.where((token >= b * dec_len) & (token < (b + 1) * dec_len), kst, 0.0).astype(BF16)
             for b in range(SAMPLE_BB)], axis=0)
        kv_all = _dot(kst_all, v)
        cross = []
        for b in range(SAMPLE_BB):
            rows = slice(b * dec_len, (b + 1) * dec_len)
            state = st_ref[b, h]
            cross.append(_dot(qs[rows].astype(BF16), state.astype(BF16)))
            sto_ref[b, h] = state * chunk_dec[h] + kv_all[b * RET_DK:(b + 1) * RET_DK]
        o = intra + jnp.concatenate(cross, axis=0)
        o = o * lax.rsqrt(jnp.mean(o * o, axis=-1, keepdims=True) + RMS_EPS)
        gate = _silu(z_ref[:, _G0 + h * LANES:_G0 + (h + 1) * LANES])
        cat_ref[:, lanes] = (gate * o).astype(BF16)

    causal = causal_ref[...] > 0.0
    for gi in range(GM_GROUPS):
        u, gvn = _gmlp_branch(z_ref, lng_ref, lnb_ref, gi)
        gvn_ref[:, gi * LANES:(gi + 1) * LANES] = gvn
        w = jnp.where(causal, ws_ref[gi], 0.0).astype(BF16)
        mixed = _dot(w, gvn.astype(BF16)) + bs_ref[gi]
        cat_ref[:, 512 + gi * LANES:512 + (gi + 1) * LANES] = (u * mixed).astype(BF16)

    o_ref[...] = x + _dot(cat_ref[...], wo_ref[...])


def _ab_sample(x, g, w_in, w_o, w_s, b_s, ln_g, ln_b, state, dec_batch, dec_len):
    rows_n = SAMPLE_BB * dec_len
    assert rows_n == LANES and dec_batch % SAMPLE_BB == 0
    decay, q_dec, k_dec, chunk_dec = _retention_tables(dec_len)
    cq, sq, ck, sk = (jnp.tile(t, (SAMPLE_BB, 1)) for t in
                      _rotary_tables(PAST_LEN + jnp.arange(dec_len, dtype=jnp.int32)))
    eye = jnp.eye(SAMPLE_BB, dtype=F32)
    bd_decay = jax.vmap(lambda d: jnp.kron(eye, d))(decay)
    rows_b = lambda t: jnp.broadcast_to(jnp.tile(t, (1, SAMPLE_BB))[:, :, None], (t.shape[0], rows_n, LANES))
    w_tiled = jnp.tile(w_s[:, :dec_len, :dec_len], (1, SAMPLE_BB, SAMPLE_BB))
    causal = jnp.asarray(np.kron(np.eye(SAMPLE_BB), np.tril(np.ones((dec_len, dec_len)))), F32)
    row_spec = pl.BlockSpec((rows_n, D_MODEL), lambda i: (i, 0))
    st_spec = pl.BlockSpec((SAMPLE_BB, RET_HEADS, RET_DK, RET_DV), lambda i: (i, 0, 0, 0))
    cube = (RET_HEADS, rows_n, LANES)
    tab = (rows_n, LANES)
    return pl.pallas_call(
        functools.partial(_ab_sample_kernel, chunk_dec=chunk_dec, dec_len=dec_len),
        grid=(dec_batch // SAMPLE_BB,),
        in_specs=[
            row_spec,
            _const_spec((1, D_MODEL)),
            _const_spec((D_MODEL, AB_IN)),
            _const_spec((AB_OUT, D_MODEL)),
            _const_spec(tab), _const_spec(tab), _const_spec(tab), _const_spec(tab),
            _const_spec(cube), _const_spec(cube), _const_spec(cube),
            _const_spec(cube), _const_spec(cube),
            _const_spec((GM_GROUPS, GM_GROUP_DIM)), _const_spec((GM_GROUPS, GM_GROUP_DIM)),
            _const_spec(tab),
            st_spec,
        ],
        out_specs=[
            row_spec,
            st_spec,
            pl.BlockSpec((rows_n, GM_GROUPS * GM_GROUP_DIM), lambda i: (i, 0)),
        ],
        out_shape=[
            jax.ShapeDtypeStruct((dec_batch * dec_len, D_MODEL), F32),
            jax.ShapeDtypeStruct(state.shape, F32),
            jax.ShapeDtypeStruct((dec_batch * dec_len, GM_GROUPS * GM_GROUP_DIM), F32),
        ],
        scratch_shapes=[
            pltpu.VMEM((rows_n, AB_IN), F32),
            pltpu.VMEM((rows_n, AB_OUT), BF16),
        ],
        compiler_params=_params("arbitrary"),
        name="ab_sample",
    )(x, g, w_in, w_o, cq, sq, ck, sk, bd_decay, rows_b(q_dec), rows_b(k_dec),
      w_tiled, rows_b(b_s[:, :dec_len]), ln_g, ln_b, causal, state)


_LOG2E = float(np.log2(np.e))


def _attend(lhs, ktop, kbot, vtop, vbot, bias, sink_even, sink_odd):
    s = _dot_nt(lhs, jnp.concatenate([ktop, kbot], axis=0)) + bias
    s0, s1, s2, s3 = (s[:, i * LANES:(i + 1) * LANES] for i in range(4))
    m_even = jnp.max(jnp.maximum(jnp.maximum(s0, s1), sink_even), axis=-1, keepdims=True)
    m_odd = jnp.max(jnp.maximum(jnp.maximum(s2, s3), sink_odd), axis=-1, keepdims=True)
    p = jnp.concatenate([jnp.exp2(s0 - m_even), jnp.exp2(s1 - m_even),
                         jnp.exp2(s2 - m_odd), jnp.exp2(s3 - m_odd)], axis=1)
    out = _dot(p.astype(BF16), jnp.concatenate([vtop, vbot], axis=0))
    even_lane = lax.broadcasted_iota(jnp.int32, sink_even.shape, 1) < SWA_HD
    sink_term = jnp.where(even_lane, jnp.exp2(sink_even - m_even), jnp.exp2(sink_odd - m_odd))
    return out[:, :LANES] / (out[:, LANES:] + sink_term)


def _band_bias(q_rows, first_key_row):
    i = np.arange(q_rows)[:, None]
    j = np.arange(2 * WINDOW)[None, :]
    rel = i + first_key_row - j
    half = np.where((rel >= 0) & (rel <= WINDOW), 0.0, -np.inf).astype(np.float32)
    return np.concatenate([half, half], axis=1)


def _kv_layouts(kcol, vcol):
    even_lane = lax.broadcasted_iota(jnp.int32, kcol.shape, 1) < SWA_HD
    ks = kcol * (SWA_HD ** -0.5 * _LOG2E)
    kr = pltpu.roll(ks, SWA_HD, 1)
    vr = pltpu.roll(vcol, SWA_HD, 1)
    one_even = jnp.where(even_lane, 1.0, 0.0)
    one_odd = 1.0 - one_even
    zero = jnp.zeros_like(kcol)
    out = []
    for kvh in range(SWA_KV_HEADS):
        k_lo, k_hi = (ks, kr) if kvh == 0 else (kr, ks)
        v_lo, v_hi = (vcol, vr) if kvh == 0 else (vr, vcol)
        ktop = jnp.where(even_lane, k_lo, zero).astype(BF16)
        kbot = jnp.where(even_lane, zero, k_hi).astype(BF16)
        vtop = jnp.concatenate([jnp.where(even_lane, v_lo, zero), one_even], axis=1).astype(BF16)
        vbot = jnp.concatenate([jnp.where(even_lane, zero, v_hi), one_odd], axis=1).astype(BF16)
        out.append((ktop, kbot, vtop, vbot))
    return out


def _swa_prompt_kernel(x_ref, xh_ref, g_ref, wqkv_ref, bqkv_ref, wo_ref, bo_ref, sink_ref, bias_ref,
                       o_ref, kc_ref, vc_ref, z_ref, cat_ref):
    si = pl.program_id(1)
    nblk = MIXER_ROWS // WINDOW
    npair = SWA_HEADS // SWA_KV_HEADS // 2
    half = MIXER_ROWS // 2

    x = x_ref[...]
    hn = _rms(x, g_ref[...]).astype(BF16)
    z_ref[:half, :] = _dot(hn[:half], wqkv_ref[...]) + bqkv_ref[...]
    z_ref[half:, :] = _dot(hn[half:], wqkv_ref[...]) + bqkv_ref[...]
    hh = _rms(xh_ref[...], g_ref[...]).astype(BF16)
    zh = _dot(hh, wqkv_ref[:, _SK0:]) + bqkv_ref[:, _SK0:]

    kcol = z_ref[:, _SK0:_SK0 + LANES]
    vcol = z_ref[:, _SV0:_SV0 + LANES]
    kc_ref[...] = kcol[MIXER_ROWS - WINDOW:]
    vc_ref[...] = vcol[MIXER_ROWS - WINDOW:]
    layouts = _kv_layouts(jnp.concatenate([zh[:, :LANES], kcol], axis=0),
                          jnp.concatenate([zh[:, LANES:], vcol], axis=0))

    for r in range(nblk):
        rows = slice(r * WINDOW, (r + 1) * WINDOW)
        keys = slice(r * WINDOW, (r + 2) * WINDOW)
        tile = bias_ref[jnp.where(si > 0, 0, 1)] if r == 0 else bias_ref[0]
        bias = jnp.concatenate([tile] * npair, axis=0)
        for kvh, (ktop, kbot, vtop, vbot) in enumerate(layouts):
            c0 = kvh * npair * LANES
            lhs = jnp.concatenate(
                [z_ref[rows, c0 + p * LANES:c0 + (p + 1) * LANES] for p in range(npair)], axis=0).astype(BF16)
            o = _attend(lhs, ktop[keys], kbot[keys], vtop[keys], vbot[keys], bias,
                        sink_ref[kvh, 0], sink_ref[kvh, 1])
            for p in range(npair):
                cat_ref[rows, c0 + p * LANES:c0 + (p + 1) * LANES] = o[p * WINDOW:(p + 1) * WINDOW].astype(BF16)

    o_ref[...] = x + _dot(cat_ref[...], wo_ref[...]) + bo_ref[...]


def _sink_table(sinks, rows_per_pair):
    npair = SWA_HEADS // SWA_KV_HEADS // 2
    s = (sinks.astype(F32) * _LOG2E).reshape(SWA_KV_HEADS, npair, 2).transpose(0, 2, 1)[:, :, :, None, None]
    s = jnp.broadcast_to(s, (SWA_KV_HEADS, 2, npair, rows_per_pair, LANES))
    return s.reshape(SWA_KV_HEADS, 2, npair * rows_per_pair, LANES)


def _swa_prompt(x, g, w_qkv, b_qkv, sinks, w_o, b_o, batch, seq):
    nstep = seq // MIXER_ROWS
    npair = SWA_HEADS // SWA_KV_HEADS // 2
    row_spec = pl.BlockSpec((MIXER_ROWS, D_MODEL), lambda b, s: (b * nstep + s, 0))
    nblk = MIXER_ROWS // WINDOW
    halo_spec = pl.BlockSpec((WINDOW, D_MODEL), lambda b, s: (jnp.maximum((b * nstep + s) * nblk - 1, 0), 0))
    cache_spec = pl.BlockSpec((None, WINDOW, LANES), lambda b, s: (b, 0, 0))
    band = _band_bias(WINDOW, WINDOW)
    no_prev = np.where(np.arange(4 * WINDOW)[None, :] % (2 * WINDOW) >= WINDOW, band, -np.inf).astype(np.float32)
    bias = jnp.asarray(np.stack([band, no_prev]))
    return pl.pallas_call(
        _swa_prompt_kernel,
        grid=(batch, nstep),
        in_specs=[
            row_spec,
            halo_spec,
            _const_spec((1, D_MODEL)),
            _const_spec((D_MODEL, SWA_IN)),
            _const_spec((1, SWA_IN)),
            _const_spec((SWA_OUT, D_MODEL)),
            _const_spec((1, D_MODEL)),
            _const_spec((SWA_KV_HEADS, 2, npair * WINDOW, LANES)),
            _const_spec((2, WINDOW, 4 * WINDOW)),
        ],
        out_specs=[row_spec, cache_spec, cache_spec],
        out_shape=[
            jax.ShapeDtypeStruct((batch * seq, D_MODEL), F32),
            jax.ShapeDtypeStruct((batch, WINDOW, LANES), F32),
            jax.ShapeDtypeStruct((batch, WINDOW, LANES), F32),
        ],
        scratch_shapes=[
            pltpu.VMEM((MIXER_ROWS, SWA_IN), F32),
            pltpu.VMEM((MIXER_ROWS, SWA_OUT), BF16),
        ],
        compiler_params=_params("arbitrary", "arbitrary"),
        name="swa_prompt",
    )(x, x, g, w_qkv, b_qkv, w_o, b_o, _sink_table(sinks, WINDOW), bias)


def _swa_sample_kernel(x_ref, g_ref, wqkv_ref, bqkv_ref, wo_ref, bo_ref, sink_ref, bias_ref, ck_ref, cv_ref,
                       o_ref, nk_ref, nv_ref, z_ref, cat_ref, *, dec_len):
    npair = SWA_HEADS // SWA_KV_HEADS // 2
    x = x_ref[...]
    hn = _rms(x, g_ref[...]).astype(BF16)
    z_ref[...] = _dot(hn, wqkv_ref[...]) + bqkv_ref[...]

    bias = bias_ref[...]
    pad = jnp.zeros((WINDOW - dec_len, LANES), F32)
    for b in range(SAMPLE_BB):
        rows = slice(b * dec_len, (b + 1) * dec_len)
        knew = z_ref[rows, _SK0:_SK0 + LANES]
        vnew = z_ref[rows, _SV0:_SV0 + LANES]
        kold, vold = ck_ref[b], cv_ref[b]
        nk_ref[b, 0:WINDOW - dec_len, :] = kold[dec_len:]
        nk_ref[b, WINDOW - dec_len:, :] = knew
        nv_ref[b, 0:WINDOW - dec_len, :] = vold[dec_len:]
        nv_ref[b, WINDOW - dec_len:, :] = vnew
        kall = jnp.concatenate([kold, knew, pad], axis=0)
        vall = jnp.concatenate([vold, vnew, pad], axis=0)
        for kvh, (ktop, kbot, vtop, vbot) in enumerate(_kv_layouts(kall, vall)):
            c0 = kvh * npair * LANES
            lhs = jnp.concatenate(
                [z_ref[rows, c0 + p * LANES:c0 + (p + 1) * LANES] for p in range(npair)], axis=0).astype(BF16)
            o = _attend(lhs, ktop, kbot, vtop, vbot, bias, sink_ref[kvh, 0], sink_ref[kvh, 1])
            for p in range(npair):
                cat_ref[rows, c0 + p * LANES:c0 + (p + 1) * LANES] = o[p * dec_len:(p + 1) * dec_len]

    o_ref[...] = x + _dot(cat_ref[...].astype(BF16), wo_ref[...]) + bo_ref[...]


def _swa_sample(x, g, w_qkv, b_qkv, sinks, w_o, b_o, cache_k, cache_v, dec_batch, dec_len):
    rows_n = SAMPLE_BB * dec_len
    npair = SWA_HEADS // SWA_KV_HEADS // 2
    row_spec = pl.BlockSpec((rows_n, D_MODEL), lambda i: (i, 0))
    cache_spec = pl.BlockSpec((SAMPLE_BB, WINDOW, LANES), lambda i: (i, 0, 0))
    return pl.pallas_call(
        functools.partial(_swa_sample_kernel, dec_len=dec_len),
        grid=(dec_batch // SAMPLE_BB,),
        in_specs=[
            row_spec,
            _const_spec((1, D_MODEL)),
            _const_spec((D_MODEL, SWA_IN)),
            _const_spec((1, SWA_IN)),
            _const_spec((SWA_OUT, D_MODEL)),
            _const_spec((1, D_MODEL)),
            _const_spec((SWA_KV_HEADS, 2, npair * dec_len, LANES)),
            _const_spec((npair * dec_len, 4 * WINDOW)),
            cache_spec, cache_spec,
        ],
        out_specs=[row_spec, cache_spec, cache_spec],
        out_shape=[
            jax.ShapeDtypeStruct((dec_batch * dec_len, D_MODEL), F32),
            jax.ShapeDtypeStruct((dec_batch, WINDOW, LANES), F32),
            jax.ShapeDtypeStruct((dec_batch, WINDOW, LANES), F32),
        ],
        scratch_shapes=[
            pltpu.VMEM((rows_n, SWA_IN), F32),
            pltpu.VMEM((rows_n, SWA_OUT), F32),
        ],
        compiler_params=_params("arbitrary"),
        name="swa_sample",
    )(x, g, w_qkv, b_qkv, w_o, b_o, _sink_table(sinks, dec_len),
      jnp.asarray(np.tile(_band_bias(dec_len, WINDOW), (npair, 1))), cache_k, cache_v)


def kernel(x_prompt, x_sample, state_ret, cache_swa_k, cache_swa_v, norm_mix, norm_mlp, norm_final, ab_w_in, ab_w_s, ab_b_s, ab_ln_g, ab_ln_b, ab_w_o, swa_w_qkv, swa_b_qkv, swa_sinks, swa_w_o, swa_b_o, mlp_w_up, mlp_w_down):
    batch, seq, _ = x_prompt.shape
    dec_batch, dec_len, _ = x_sample.shape
    assert seq % MIXER_ROWS == 0 and seq % RET_CHUNK == 0
    kv_lanes = SWA_KV_HEADS * SWA_HD

    hp = x_prompt.reshape(batch * seq, D_MODEL)
    hs = x_sample.reshape(dec_batch * dec_len, D_MODEL)
    row = lambda v: v.reshape(1, -1)
    gfin = row(norm_final)

    w_in, w_o = ab_w_in[0].astype(BF16), ab_w_o[0].astype(BF16)
    hp, ret_p = _ab_prompt(hp, row(norm_mix[0]), w_in, w_o, ab_w_s[0], ab_b_s[0], ab_ln_g[0], ab_ln_b[0],
                           batch, seq)
    hs, ret_s, gm_s = _ab_sample(hs, row(norm_mix[0]), w_in, w_o, ab_w_s[0], ab_b_s[0], ab_ln_g[0], ab_ln_b[0],
                                 state_ret[0], dec_batch, dec_len)
    w_up, w_dn = mlp_w_up[0].astype(BF16), mlp_w_down[0].astype(BF16)
    hp = _mlp(hp, row(norm_mlp[0]), w_up, w_dn, gfin, final_norm=False, name="mlp0_prompt")
    hs = _mlp(hs, row(norm_mlp[0]), w_up, w_dn, gfin, final_norm=False, name="mlp0_sample")

    w_qkv, w_so = swa_w_qkv[0].astype(BF16), swa_w_o[0].astype(BF16)
    hp, kp, vp = _swa_prompt(hp, row(norm_mix[1]), w_qkv, row(swa_b_qkv[0]), swa_sinks[0], w_so, row(swa_b_o[0]),
                             batch, seq)
    hs, ks, vs = _swa_sample(hs, row(norm_mix[1]), w_qkv, row(swa_b_qkv[0]), swa_sinks[0], w_so, row(swa_b_o[0]),
                             cache_swa_k[0].reshape(dec_batch, WINDOW, kv_lanes),
                             cache_swa_v[0].reshape(dec_batch, WINDOW, kv_lanes), dec_batch, dec_len)
    w_up, w_dn = mlp_w_up[1].astype(BF16), mlp_w_down[1].astype(BF16)
    hp = _mlp(hp, row(norm_mlp[1]), w_up, w_dn, gfin, final_norm=True, name="mlp1_prompt")
    hs = _mlp(hs, row(norm_mlp[1]), w_up, w_dn, gfin, final_norm=True, name="mlp1_sample")

    cache_shape = lambda n: (1, n, WINDOW, SWA_KV_HEADS, SWA_HD)
    return (hp.reshape(batch, seq, D_MODEL),
            hs.reshape(dec_batch, dec_len, D_MODEL),
            ret_p[None],
            ret_s[None],
            gm_s.reshape(1, dec_batch, dec_len, GM_GROUPS * GM_GROUP_DIM),
            kp.reshape(cache_shape(batch)),
            vp.reshape(cache_shape(batch)),
            ks.reshape(cache_shape(dec_batch)),
            vs.reshape(cache_shape(dec_batch)))
```

```python
import functools

import numpy as np
import jax
import jax.numpy as jnp
from jax import lax
from jax.experimental import pallas as pl
from jax.experimental.pallas import tpu as pltpu

F32 = jnp.float32
BF16 = jnp.bfloat16

D_MODEL = 1024
PAST_LEN = 16384
RMS_EPS = 1e-6
LN_EPS = 1e-5
RET_HEADS = 4
RET_DK = 128
RET_DV = 128
RET_CHUNK = 128
ROPE_BASE = 10000.0
GM_GROUPS = 4
GM_GROUP_DIM = 128
GM_CHUNK = 128
AB_IN = 3072
AB_OUT = 1024
SWA_HEADS = 16
SWA_KV_HEADS = 2
SWA_HD = 64
WINDOW = 128
SWA_IN = 1280
SWA_OUT = 1024
D_FF = 4096

_Q0, _K0, _V0, _G0, _U0, _GV0 = 0, 512, 1024, 1536, 2048, 2560
_SK0, _SV0 = 1024, 1152

LANES = 128
MIXER_ROWS = 512
MIXER_SPLIT = 2
MLP_ROWS = 1024
MLP_SPLIT = 2
FF_CHUNK = 1024
SAMPLE_BB = 16
VMEM_LIMIT = 56 * 1024 * 1024


def _rms(x, g):
    return x * lax.rsqrt(jnp.mean(x * x, axis=-1, keepdims=True) + RMS_EPS) * g


def _gelu(x):
    return 0.5 * x * (1.0 + lax.erf(x * np.float32(np.sqrt(0.5))))


def _silu(x):
    return x / (1.0 + jnp.exp(-x))


def _dot(a, b):
    return jnp.dot(a, b, preferred_element_type=F32)


def _dot_nt(a, b):
    return lax.dot_general(a, b, (((1,), (1,)), ((), ())), preferred_element_type=F32)


def _dot_tn(a, b):
    return lax.dot_general(a, b, (((0,), (0,)), ((), ())), preferred_element_type=F32)


def _const_spec(shape):
    zeros = (0,) * len(shape)
    return pl.BlockSpec(shape, lambda *_: zeros, pipeline_mode=pl.Buffered(1))


def _params(*sem):
    return pltpu.CompilerParams(dimension_semantics=sem, vmem_limit_bytes=VMEM_LIMIT)


def _mlp_rows(x_ref, g_ref, wup_ref, wdn_ref, gf_ref, o_ref, final_norm):
    rows = x_ref.shape[0] // MLP_SPLIT
    groups = [slice(i * rows, (i + 1) * rows) for i in range(MLP_SPLIT)]
    xs = [x_ref[r, :] for r in groups]
    hns = [_rms(x, g_ref[...]).astype(BF16) for x in xs]
    accs = [jnp.zeros_like(x) for x in xs]
    for c in range(D_FF // FF_CHUNK):
        cols = slice(c * FF_CHUNK, (c + 1) * FF_CHUNK)
        for i in range(MLP_SPLIT):
            h = _dot(hns[i], wup_ref[:, cols])
            h = jnp.square(jnp.maximum(h, 0.0)).astype(BF16)
            accs[i] = accs[i] + _dot(h, wdn_ref[cols, :])
    for i, r in enumerate(groups):
        y = xs[i] + accs[i]
        if final_norm:
            y = _rms(y, gf_ref[...])
        o_ref[r, :] = y


def _mlp_kernel(xp_ref, xs_ref, g_ref, wup_ref, wdn_ref, gf_ref, op_ref, os_ref, *, final_norm, prompt_steps):
    step = pl.program_id(0)

    @pl.when(step < prompt_steps)
    def _():
        _mlp_rows(xp_ref, g_ref, wup_ref, wdn_ref, gf_ref, op_ref, final_norm)

    @pl.when(step >= prompt_steps)
    def _():
        _mlp_rows(xs_ref, g_ref, wup_ref, wdn_ref, gf_ref, os_ref, final_norm)


def _mlp(xp, xs, g, wup, wdn, gf, *, final_norm, name):
    tm = MLP_ROWS
    assert xp.shape[0] % tm == 0 and xs.shape[0] % tm == 0
    p_steps, s_steps = xp.shape[0] // tm, xs.shape[0] // tm
    p_spec = pl.BlockSpec((tm, D_MODEL), lambda i: (jnp.minimum(i, p_steps - 1), 0))
    s_spec = pl.BlockSpec((tm, D_MODEL), lambda i: (jnp.maximum(i - p_steps, 0), 0),
                          pipeline_mode=pl.Buffered(1))
    return pl.pallas_call(
        functools.partial(_mlp_kernel, final_norm=final_norm, prompt_steps=p_steps),
        grid=(p_steps + s_steps,),
        in_specs=[
            p_spec,
            s_spec,
            _const_spec((1, D_MODEL)),
            _const_spec((D_MODEL, D_FF)),
            _const_spec((D_FF, D_MODEL)),
            _const_spec((1, D_MODEL)),
        ],
        out_specs=[p_spec, s_spec],
        out_shape=[jax.ShapeDtypeStruct(xp.shape, F32), jax.ShapeDtypeStruct(xs.shape, F32)],
        compiler_params=_params("arbitrary"),
        name=name,
    )(xp, xs, g, wup, wdn, gf)


def _rotary(x, cos2, sin2):
    return x * cos2 + pltpu.roll(x, 64, 1) * sin2


def _retention_tables(chunk):
    log_g = jnp.log1p(-jnp.exp2(-5.0 - jnp.arange(RET_HEADS, dtype=F32)))
    idx = jnp.arange(chunk, dtype=F32)
    diff = idx[:, None] - idx[None, :]
    decay = jnp.where(diff[None] >= 0, jnp.exp(log_g[:, None, None] * jnp.maximum(diff, 0.0)[None]), 0.0)
    q_dec = jnp.exp(log_g[:, None] * (idx[None, :] + 1.0))
    k_dec = jnp.exp(log_g[:, None] * (chunk - 1.0 - idx[None, :]))
    log_g64 = np.log1p(-np.exp2(-5.0 - np.arange(RET_HEADS, dtype=np.float64)))
    chunk_dec = tuple(float(v) for v in np.exp(log_g64 * chunk))
    return decay, q_dec, k_dec, chunk_dec


def _rotary_tables(pos):
    half = RET_DK // 2
    inv = ROPE_BASE ** (-jnp.arange(half, dtype=F32) / half)
    ang = pos.astype(F32)[:, None] * inv[None, :]
    cos, sin = jnp.cos(ang), jnp.sin(ang)
    cos2 = jnp.concatenate([cos, cos], axis=-1)
    sin2 = jnp.concatenate([-sin, sin], axis=-1)
    scale = RET_DK ** -0.5
    return cos2 * scale, sin2 * scale, cos2, sin2


def _gmlp_branch(z_ref, rows, lng_ref, lnb_ref, gi):
    u = _gelu(z_ref[rows, _U0 + gi * LANES:_U0 + (gi + 1) * LANES])
    gv = _gelu(z_ref[rows, _GV0 + gi * LANES:_GV0 + (gi + 1) * LANES])
    mu = jnp.mean(gv, axis=-1, keepdims=True)
    cen = gv - mu
    var = jnp.mean(cen * cen, axis=-1, keepdims=True)
    gvn = cen * lax.rsqrt(var + LN_EPS) * lng_ref[gi:gi + 1, :] + lnb_ref[gi:gi + 1, :]
    return u, gvn


def _ab_prompt_kernel(x_ref, g_ref, win_ref, wo_ref, cq_ref, sq_ref, ck_ref, sk_ref,
                      dec_ref, qd_ref, kd_ref, ws_ref, bs_ref, lng_ref, lnb_ref,
                      o_ref, st_ref, *scratch, chunk_dec):
    si = pl.program_id(1)
    z_refs, cat_refs, s_ref = scratch[:MIXER_SPLIT], scratch[MIXER_SPLIT:2 * MIXER_SPLIT], scratch[-1]
    sub_rows = MIXER_ROWS // MIXER_SPLIT
    nchunk = sub_rows // RET_CHUNK

    @pl.when(si == 0)
    def _():
        s_ref[...] = jnp.zeros_like(s_ref)

    ri = lax.broadcasted_iota(jnp.int32, (GM_CHUNK, GM_CHUNK), 0)
    ci = lax.broadcasted_iota(jnp.int32, (GM_CHUNK, GM_CHUNK), 1)
    w_spatial = [jnp.where(ci <= ri, ws_ref[gi], 0.0).astype(BF16) for gi in range(GM_GROUPS)]
    tile_rows = lambda t: jnp.concatenate([t] * nchunk, axis=0)
    states = [s_ref[h] for h in range(RET_HEADS)]

    for sub in range(MIXER_SPLIT):
        rows = slice(sub * sub_rows, (sub + 1) * sub_rows)
        chunks = [slice(c * RET_CHUNK, (c + 1) * RET_CHUNK) for c in range(nchunk)]
        z_ref, cat_ref = z_refs[sub], cat_refs[sub]
        x = x_ref[rows, :]
        hn = _rms(x, g_ref[...]).astype(BF16)
        z_ref[:, :_U0] = _dot(hn, win_ref[:, :_U0])
        z_ref[:, _U0:] = _dot(hn, win_ref[:, _U0:])

        cq, sq, ck, sk = cq_ref[rows, :], sq_ref[rows, :], ck_ref[rows, :], sk_ref[rows, :]
        stage1 = []
        for h in range(RET_HEADS):
            q = _rotary(z_ref[:, _Q0 + h * LANES:_Q0 + (h + 1) * LANES], cq, sq)
            k = _rotary(z_ref[:, _K0 + h * LANES:_K0 + (h + 1) * LANES], ck, sk)
            v = z_ref[:, _V0 + h * LANES:_V0 + (h + 1) * LANES].astype(BF16)
            qb, kb = q.astype(BF16), k.astype(BF16)
            qdb = (q * tile_rows(qd_ref[h])).astype(BF16)
            kdb = (k * tile_rows(kd_ref[h])).astype(BF16)
            local = [slice(c * RET_CHUNK, (c + 1) * RET_CHUNK) for c in range(nchunk)]
            scores = [_dot_nt(qb[r], kb[r]) for r in local]
            kvs = [_dot_tn(kdb[r], v[r]) for r in local]
            before = []
            for kv in kvs:
                before.append(states[h].astype(BF16))
                states[h] = states[h] * chunk_dec[h] + kv
            stage1.append((v, qdb, scores, before, local))
        for h, (v, qdb, scores, before, local) in enumerate(stage1):
            lanes = slice(h * LANES, (h + 1) * LANES)
            dec = dec_ref[h]
            for c, r in enumerate(local):
                lhs = jnp.concatenate([(scores[c] * dec).astype(BF16), qdb[r]], axis=1)
                rhs = jnp.concatenate([v[r], before[c]], axis=0)
                o = _dot(lhs, rhs)
                o = o * lax.rsqrt(jnp.mean(o * o, axis=-1, keepdims=True) + RMS_EPS)
                gate = _silu(z_ref[chunks[c], _G0 + h * LANES:_G0 + (h + 1) * LANES])
                cat_ref[chunks[c], lanes] = (gate * o).astype(BF16)

        for gi in range(GM_GROUPS):
            u, gvn = _gmlp_branch(z_ref, slice(None), lng_ref, lnb_ref, gi)
            gvb = gvn.astype(BF16)
            rhs = jnp.concatenate([gvb[c * GM_CHUNK:(c + 1) * GM_CHUNK] for c in range(nchunk)], axis=1)
            mixed = _dot(w_spatial[gi], rhs)
            bias = bs_ref[gi]
            for c in range(nchunk):
                local_rows = slice(c * GM_CHUNK, (c + 1) * GM_CHUNK)
                m = mixed[:, c * GM_CHUNK:(c + 1) * GM_CHUNK] + bias
                cat_ref[chunks[c], 512 + gi * LANES:512 + (gi + 1) * LANES] = (u[local_rows] * m).astype(BF16)

        o_ref[rows, :] = x + _dot(cat_ref[...], wo_ref[...])

    for h in range(RET_HEADS):
        s_ref[h] = states[h]

    @pl.when(si == pl.num_programs(1) - 1)
    def _():
        st_ref[...] = s_ref[...]


def _ab_prompt(x, g, w_in, w_o, w_s, b_s, ln_g, ln_b, batch, seq):
    nstep = seq // MIXER_ROWS
    decay, q_dec, k_dec, chunk_dec = _retention_tables(RET_CHUNK)
    cq, sq, ck, sk = _rotary_tables(jnp.arange(seq, dtype=jnp.int32))
    lane_b = lambda t: jnp.broadcast_to(t[:, :, None], t.shape + (LANES,))
    row_spec = pl.BlockSpec((MIXER_ROWS, D_MODEL), lambda b, s: (b * nstep + s, 0))
    tab_spec = pl.BlockSpec((MIXER_ROWS, LANES), lambda b, s: (s, 0))
    cube = (RET_HEADS, RET_CHUNK, LANES)
    out, state = pl.pallas_call(
        functools.partial(_ab_prompt_kernel, chunk_dec=chunk_dec),
        grid=(batch, nstep),
        in_specs=[
            row_spec,
            _const_spec((1, D_MODEL)),
            _const_spec((D_MODEL, AB_IN)),
            _const_spec((AB_OUT, D_MODEL)),
            tab_spec, tab_spec, tab_spec, tab_spec,
            _const_spec(cube), _const_spec(cube), _const_spec(cube),
            _const_spec(cube), _const_spec(cube),
            _const_spec((GM_GROUPS, GM_GROUP_DIM)), _const_spec((GM_GROUPS, GM_GROUP_DIM)),
        ],
        out_specs=[
            row_spec,
            pl.BlockSpec((None, RET_HEADS, RET_DK, RET_DV), lambda b, s: (b, 0, 0, 0)),
        ],
        out_shape=[
            jax.ShapeDtypeStruct((batch * seq, D_MODEL), F32),
            jax.ShapeDtypeStruct((batch, RET_HEADS, RET_DK, RET_DV), F32),
        ],
        scratch_shapes=(
            [pltpu.VMEM((MIXER_ROWS // MIXER_SPLIT, AB_IN), F32)] * MIXER_SPLIT
            + [pltpu.VMEM((MIXER_ROWS // MIXER_SPLIT, AB_OUT), BF16)] * MIXER_SPLIT
            + [pltpu.VMEM((RET_HEADS, RET_DK, RET_DV), F32)]),
        compiler_params=_params("arbitrary", "arbitrary"),
        name="ab_prompt",
    )(x, g, w_in, w_o, cq, sq, ck, sk, decay, lane_b(q_dec), lane_b(k_dec),
      w_s, lane_b(b_s), ln_g, ln_b)
    return out, state


def _ab_sample_kernel(x_ref, g_ref, win_ref, wo_ref, cq_ref, sq_ref, ck_ref, sk_ref,
                      dec_ref, qd_ref, kd_ref, ws_ref, bs_ref, lng_ref, lnb_ref, causal_ref, st_ref,
                      o_ref, sto_ref, gvn_ref, z_ref, cat_ref, *, chunk_dec, dec_len):
    rows_n = SAMPLE_BB * dec_len
    x = x_ref[...]
    hn = _rms(x, g_ref[...]).astype(BF16)
    z_ref[...] = _dot(hn, win_ref[...])

    cq, sq, ck, sk = cq_ref[...], sq_ref[...], ck_ref[...], sk_ref[...]
    token = lax.broadcasted_iota(jnp.int32, (rows_n, rows_n), 1)
    for h in range(RET_HEADS):
        lanes = slice(h * LANES, (h + 1) * LANES)
        q = _rotary(z_ref[:, _Q0 + h * LANES:_Q0 + (h + 1) * LANES], cq, sq)
        k = _rotary(z_ref[:, _K0 + h * LANES:_K0 + (h + 1) * LANES], ck, sk)
        v = z_ref[:, _V0 + h * LANES:_V0 + (h + 1) * LANES].astype(BF16)
        scores = _dot_nt(q.astype(BF16), k.astype(BF16)) * dec_ref[h]
        intra = _dot(scores.astype(BF16), v)
        qs = q * qd_ref[h]
        kst = (k * kd_ref[h]).T
        kst_all = jnp.concatenate(
            [jnp.where((token >= b * dec_len) & (token < (b + 1) * dec_len), kst, 0.0).astype(BF16)
             for b in range(SAMPLE_BB)], axis=0)
        kv_all = _dot(kst_all, v)
        cross = []
        for b in range(SAMPLE_BB):
            rows = slice(b * dec_len, (b + 1) * dec_len)
            state = st_ref[b, h]
            cross.append(_dot(qs[rows].astype(BF16), state.astype(BF16)))
            sto_ref[b, h] = state * chunk_dec[h] + kv_all[b * RET_DK:(b + 1) * RET_DK]
        o = intra + jnp.concatenate(cross, axis=0)
        o = o * lax.rsqrt(jnp.mean(o * o, axis=-1, keepdims=True) + RMS_EPS)
        gate = _silu(z_ref[:, _G0 + h * LANES:_G0 + (h + 1) * LANES])
        cat_ref[:, lanes] = (gate * o).astype(BF16)

    causal = causal_ref[...] > 0.0
    for gi in range(GM_GROUPS):
        u, gvn = _gmlp_branch(z_ref, slice(None), lng_ref, lnb_ref, gi)
        gvn_ref[:, gi * LANES:(gi + 1) * LANES] = gvn
        w = jnp.where(causal, ws_ref[gi], 0.0).astype(BF16)
        mixed = _dot(w, gvn.astype(BF16)) + bs_ref[gi]
        cat_ref[:, 512 + gi * LANES:512 + (gi + 1) * LANES] = (u * mixed).astype(BF16)

    o_ref[...] = x + _dot(cat_ref[...], wo_ref[...])


def _ab_sample(x, g, w_in, w_o, w_s, b_s, ln_g, ln_b, state, dec_batch, dec_len):
    rows_n = SAMPLE_BB * dec_len
    assert rows_n == LANES and dec_batch % SAMPLE_BB == 0
    decay, q_dec, k_dec, chunk_dec = _retention_tables(dec_len)
    cq, sq, ck, sk = (jnp.tile(t, (SAMPLE_BB, 1)) for t in
                      _rotary_tables(PAST_LEN + jnp.arange(dec_len, dtype=jnp.int32)))
    eye = jnp.eye(SAMPLE_BB, dtype=F32)
    bd_decay = jax.vmap(lambda d: jnp.kron(eye, d))(decay)
    rows_b = lambda t: jnp.broadcast_to(jnp.tile(t, (1, SAMPLE_BB))[:, :, None], (t.shape[0], rows_n, LANES))
    w_tiled = jnp.tile(w_s[:, :dec_len, :dec_len], (1, SAMPLE_BB, SAMPLE_BB))
    causal = jnp.asarray(np.kron(np.eye(SAMPLE_BB), np.tril(np.ones((dec_len, dec_len)))), F32)
    row_spec = pl.BlockSpec((rows_n, D_MODEL), lambda i: (i, 0))
    st_spec = pl.BlockSpec((SAMPLE_BB, RET_HEADS, RET_DK, RET_DV), lambda i: (i, 0, 0, 0))
    cube = (RET_HEADS, rows_n, LANES)
    tab = (rows_n, LANES)
    return pl.pallas_call(
        functools.partial(_ab_sample_kernel, chunk_dec=chunk_dec, dec_len=dec_len),
        grid=(dec_batch // SAMPLE_BB,),
        in_specs=[
            row_spec,
            _const_spec((1, D_MODEL)),
            _const_spec((D_MODEL, AB_IN)),
            _const_spec((AB_OUT, D_MODEL)),
            _const_spec(tab), _const_spec(tab), _const_spec(tab), _const_spec(tab),
            _const_spec(cube), _const_spec(cube), _const_spec(cube),
            _const_spec(cube), _const_spec(cube),
            _const_spec((GM_GROUPS, GM_GROUP_DIM)), _const_spec((GM_GROUPS, GM_GROUP_DIM)),
            _const_spec(tab),
            st_spec,
        ],
        out_specs=[
            row_spec,
            st_spec,
            pl.BlockSpec((rows_n, GM_GROUPS * GM_GROUP_DIM), lambda i: (i, 0)),
        ],
        out_shape=[
            jax.ShapeDtypeStruct((dec_batch * dec_len, D_MODEL), F32),
            jax.ShapeDtypeStruct(state.shape, F32),
            jax.ShapeDtypeStruct((dec_batch * dec_len, GM_GROUPS * GM_GROUP_DIM), F32),
        ],
        scratch_shapes=[
            pltpu.VMEM((rows_n, AB_IN), F32),
            pltpu.VMEM((rows_n, AB_OUT), BF16),
        ],
        compiler_params=_params("arbitrary"),
        name="ab_sample",
    )(x, g, w_in, w_o, cq, sq, ck, sk, bd_decay, rows_b(q_dec), rows_b(k_dec),
      w_tiled, rows_b(b_s[:, :dec_len]), ln_g, ln_b, causal, state)


_LOG2E = float(np.log2(np.e))


def _softmax_numerators(s, sink_even, sink_odd):
    s0, s1, s2, s3 = (s[:, i * LANES:(i + 1) * LANES] for i in range(4))
    m_even = jnp.max(jnp.maximum(jnp.maximum(s0, s1), sink_even), axis=-1, keepdims=True)
    m_odd = jnp.max(jnp.maximum(jnp.maximum(s2, s3), sink_odd), axis=-1, keepdims=True)
    p = jnp.concatenate([jnp.exp2(s0 - m_even), jnp.exp2(s1 - m_even),
                         jnp.exp2(s2 - m_odd), jnp.exp2(s3 - m_odd)], axis=1)
    even_lane = lax.broadcasted_iota(jnp.int32, sink_even.shape, 1) < SWA_HD
    sink_term = jnp.where(even_lane, jnp.exp2(sink_even - m_even), jnp.exp2(sink_odd - m_odd))
    return p.astype(BF16), sink_term


def _band_bias(q_rows, first_key_row):
    i = np.arange(q_rows)[:, None]
    j = np.arange(2 * WINDOW)[None, :]
    rel = i + first_key_row - j
    half = np.where((rel >= 0) & (rel <= WINDOW), 0.0, -np.inf).astype(np.float32)
    return np.concatenate([half, half], axis=1)


def _kv_layouts(kcol, vcol):
    even_lane = lax.broadcasted_iota(jnp.int32, kcol.shape, 1) < SWA_HD
    ks = kcol * (SWA_HD ** -0.5 * _LOG2E)
    kr = pltpu.roll(ks, SWA_HD, 1)
    vr = pltpu.roll(vcol, SWA_HD, 1)
    one_even = jnp.where(even_lane, 1.0, 0.0)
    one_odd = 1.0 - one_even
    zero = jnp.zeros_like(kcol)
    out = []
    for kvh in range(SWA_KV_HEADS):
        k_lo, k_hi = (ks, kr) if kvh == 0 else (kr, ks)
        v_lo, v_hi = (vcol, vr) if kvh == 0 else (vr, vcol)
        ktop = jnp.where(even_lane, k_lo, zero).astype(BF16)
        kbot = jnp.where(even_lane, zero, k_hi).astype(BF16)
        vtop = jnp.concatenate([jnp.where(even_lane, v_lo, zero), one_even], axis=1).astype(BF16)
        vbot = jnp.concatenate([jnp.where(even_lane, zero, v_hi), one_odd], axis=1).astype(BF16)
        out.append((ktop, kbot, vtop, vbot))
    return out


def _swa_prompt_kernel(x_ref, xh_ref, g_ref, wqkv_ref, bqkv_ref, wo_ref, bo_ref, sink_ref, bias_ref,
                       o_ref, kc_ref, vc_ref, z_ref, cat_ref):
    si = pl.program_id(1)
    nblk = MIXER_ROWS // WINDOW
    npair = SWA_HEADS // SWA_KV_HEADS // 2
    half = MIXER_ROWS // 2

    x = x_ref[...]
    hn = _rms(x, g_ref[...]).astype(BF16)
    z_ref[:half, :] = _dot(hn[:half], wqkv_ref[...]) + bqkv_ref[...]
    z_ref[half:, :] = _dot(hn[half:], wqkv_ref[...]) + bqkv_ref[...]
    hh = _rms(xh_ref[...], g_ref[...]).astype(BF16)
    zh = _dot(hh, wqkv_ref[:, _SK0:]) + bqkv_ref[:, _SK0:]

    kcol = z_ref[:, _SK0:_SK0 + LANES]
    vcol = z_ref[:, _SV0:_SV0 + LANES]
    kc_ref[...] = kcol[MIXER_ROWS - WINDOW:]
    vc_ref[...] = vcol[MIXER_ROWS - WINDOW:]
    layouts = _kv_layouts(jnp.concatenate([zh[:, :LANES], kcol], axis=0),
                          jnp.concatenate([zh[:, LANES:], vcol], axis=0))

    for r in range(nblk):
        rows = slice(r * WINDOW, (r + 1) * WINDOW)
        keys = slice(r * WINDOW, (r + 2) * WINDOW)
        tile = bias_ref[jnp.where(si > 0, 0, 1)] if r == 0 else bias_ref[0]
        bias = jnp.concatenate([tile] * npair, axis=0)
        for kvh, (ktop, kbot, vtop, vbot) in enumerate(layouts):
            c0 = kvh * npair * LANES
            lhs = jnp.concatenate(
                [z_ref[rows, c0 + p * LANES:c0 + (p + 1) * LANES] for p in range(npair)], axis=0).astype(BF16)
            s = _dot_nt(lhs, jnp.concatenate([ktop[keys], kbot[keys]], axis=0)) + bias
            p, sink_term = _softmax_numerators(s, sink_ref[kvh, 0], sink_ref[kvh, 1])
            out = _dot(p, jnp.concatenate([vtop[keys], vbot[keys]], axis=0))
            o = out[:, :LANES] / (out[:, LANES:] + sink_term)
            for p_i in range(npair):
                cat_ref[rows, c0 + p_i * LANES:c0 + (p_i + 1) * LANES] = (
                    o[p_i * WINDOW:(p_i + 1) * WINDOW].astype(BF16))

    o_ref[...] = x + _dot(cat_ref[...], wo_ref[...]) + bo_ref[...]


def _sink_table(sinks, rows_per_pair):
    npair = SWA_HEADS // SWA_KV_HEADS // 2
    s = (sinks.astype(F32) * _LOG2E).reshape(SWA_KV_HEADS, npair, 2).transpose(0, 2, 1)[:, :, :, None, None]
    s = jnp.broadcast_to(s, (SWA_KV_HEADS, 2, npair, rows_per_pair, LANES))
    return s.reshape(SWA_KV_HEADS, 2, npair * rows_per_pair, LANES)


def _swa_prompt(x, g, w_qkv, b_qkv, sinks, w_o, b_o, batch, seq):
    nstep = seq // MIXER_ROWS
    npair = SWA_HEADS // SWA_KV_HEADS // 2
    row_spec = pl.BlockSpec((MIXER_ROWS, D_MODEL), lambda b, s: (b * nstep + s, 0))
    nblk = MIXER_ROWS // WINDOW
    halo_spec = pl.BlockSpec((WINDOW, D_MODEL), lambda b, s: (jnp.maximum((b * nstep + s) * nblk - 1, 0), 0))
    cache_spec = pl.BlockSpec((None, WINDOW, LANES), lambda b, s: (b, 0, 0))
    band = _band_bias(WINDOW, WINDOW)
    no_prev = np.where(np.arange(4 * WINDOW)[None, :] % (2 * WINDOW) >= WINDOW, band, -np.inf).astype(np.float32)
    bias = jnp.asarray(np.stack([band, no_prev]))
    return pl.pallas_call(
        _swa_prompt_kernel,
        grid=(batch, nstep),
        in_specs=[
            row_spec,
            halo_spec,
            _const_spec((1, D_MODEL)),
            _const_spec((D_MODEL, SWA_IN)),
            _const_spec((1, SWA_IN)),
            _const_spec((SWA_OUT, D_MODEL)),
            _const_spec((1, D_MODEL)),
            _const_spec((SWA_KV_HEADS, 2, npair * WINDOW, LANES)),
            _const_spec((2, WINDOW, 4 * WINDOW)),
        ],
        out_specs=[row_spec, cache_spec, cache_spec],
        out_shape=[
            jax.ShapeDtypeStruct((batch * seq, D_MODEL), F32),
            jax.ShapeDtypeStruct((batch, WINDOW, LANES), F32),
            jax.ShapeDtypeStruct((batch, WINDOW, LANES), F32),
        ],
        scratch_shapes=[
            pltpu.VMEM((MIXER_ROWS, SWA_IN), F32),
            pltpu.VMEM((MIXER_ROWS, SWA_OUT), BF16),
        ],
        compiler_params=_params("arbitrary", "arbitrary"),
        name="swa_prompt",
    )(x, x, g, w_qkv, b_qkv, w_o, b_o, _sink_table(sinks, WINDOW), bias)


def _swa_sample_kernel(x_ref, g_ref, wqkv_ref, bqkv_ref, wo_ref, bo_ref, sink_ref, bias_ref, ck_ref, cv_ref,
                       o_ref, nk_ref, nv_ref, z_ref, cat_ref, *, dec_len):
    ncol = SWA_HEADS // 2
    grp = ncol * dec_len
    x = x_ref[...]
    hn = _rms(x, g_ref[...]).astype(BF16)
    z_ref[...] = _dot(hn, wqkv_ref[...]) + bqkv_ref[...]

    bias, sink = bias_ref[...], sink_ref[...]
    lane_lo = lax.broadcasted_iota(jnp.int32, (grp, LANES), 1) < SWA_HD
    kv0 = lax.broadcasted_iota(jnp.int32, (grp, LANES), 0) < grp // 2
    own_half = jnp.where(kv0, 0, 1) == jnp.where(lane_lo, 0, 1)
    pad = jnp.zeros((WINDOW - dec_len, LANES), F32)
    elems = [slice(b * dec_len, (b + 1) * dec_len) for b in range(SAMPLE_BB)]
    values, scores = [], []
    for b, rows in enumerate(elems):
        knew = z_ref[rows, _SK0:_SK0 + LANES]
        vnew = z_ref[rows, _SV0:_SV0 + LANES]
        kold, vold = ck_ref[b], cv_ref[b]
        nk_ref[b, 0:WINDOW - dec_len, :] = kold[dec_len:]
        nk_ref[b, WINDOW - dec_len:, :] = knew
        nv_ref[b, 0:WINDOW - dec_len, :] = vold[dec_len:]
        nv_ref[b, WINDOW - dec_len:, :] = vnew
        kall = jnp.concatenate([kold, knew, pad], axis=0).astype(BF16)
        values.append(jnp.concatenate([vold, vnew, pad], axis=0).astype(BF16))

        q = jnp.concatenate([z_ref[rows, j * LANES:(j + 1) * LANES] for j in range(ncol)], axis=0)
        q = q * (SWA_HD ** -0.5 * _LOG2E)
        q_swapped = pltpu.roll(q, SWA_HD, 1)
        q_even = jnp.where(own_half, jnp.where(kv0, q, q_swapped), 0.0)
        q_odd = jnp.where(own_half, jnp.where(kv0, q_swapped, q), 0.0)
        lhs = jnp.concatenate([q_even, q_odd], axis=0).astype(BF16)
        scores.append(_dot_nt(lhs, kall) + bias)

    probs, dens = [], []
    for s in scores:
        m = jnp.max(jnp.maximum(jnp.maximum(s[:, :LANES], s[:, LANES:]), sink), axis=-1, keepdims=True)
        p = jnp.exp2(s - m)
        dens.append(jnp.sum(p, axis=-1, keepdims=True) + jnp.exp2(sink - m))
        probs.append(p.astype(BF16))

    outs = [_dot(p, v) for p, v in zip(probs, values)]

    for rows, out, den in zip(elems, outs, dens):
        o = out / den
        o_swapped = pltpu.roll(o, SWA_HD, 1)
        cols = jnp.where(lane_lo,
                         jnp.where(kv0, o[:grp], o_swapped[:grp]),
                         jnp.where(kv0, o_swapped[grp:], o[grp:]))
        for j in range(ncol):
            cat_ref[rows, j * LANES:(j + 1) * LANES] = cols[j * dec_len:(j + 1) * dec_len]

    o_ref[...] = x + _dot(cat_ref[...].astype(BF16), wo_ref[...]) + bo_ref[...]


def _swa_sample(x, g, w_qkv, b_qkv, sinks, w_o, b_o, cache_k, cache_v, dec_batch, dec_len):
    rows_n = SAMPLE_BB * dec_len
    ncol = SWA_HEADS // 2
    score_rows = 2 * ncol * dec_len
    sink = jnp.broadcast_to((sinks.astype(F32) * _LOG2E).reshape(ncol, 2).T[:, :, None, None],
                            (2, ncol, dec_len, LANES)).reshape(score_rows, LANES)
    bias = jnp.asarray(np.tile(_band_bias(dec_len, WINDOW)[:, :2 * WINDOW], (2 * ncol, 1)))
    row_spec = pl.BlockSpec((rows_n, D_MODEL), lambda i: (i, 0))
    cache_spec = pl.BlockSpec((SAMPLE_BB, WINDOW, LANES), lambda i: (i, 0, 0))
    return pl.pallas_call(
        functools.partial(_swa_sample_kernel, dec_len=dec_len),
        grid=(dec_batch // SAMPLE_BB,),
        in_specs=[
            row_spec,
            _const_spec((1, D_MODEL)),
            _const_spec((D_MODEL, SWA_IN)),
            _const_spec((1, SWA_IN)),
            _const_spec((SWA_OUT, D_MODEL)),
            _const_spec((1, D_MODEL)),
            _const_spec((score_rows, LANES)),
            _const_spec((score_rows, 2 * WINDOW)),
            cache_spec, cache_spec,
        ],
        out_specs=[row_spec, cache_spec, cache_spec],
        out_shape=[
            jax.ShapeDtypeStruct((dec_batch * dec_len, D_MODEL), F32),
            jax.ShapeDtypeStruct((dec_batch, WINDOW, LANES), F32),
            jax.ShapeDtypeStruct((dec_batch, WINDOW, LANES), F32),
        ],
        scratch_shapes=[
            pltpu.VMEM((rows_n, SWA_IN), F32),
            pltpu.VMEM((rows_n, SWA_OUT), F32),
        ],
        compiler_params=_params("arbitrary"),
        name="swa_sample",
    )(x, g, w_qkv, b_qkv, w_o, b_o, sink, bias, cache_k, cache_v)


def kernel(x_prompt, x_sample, state_ret, cache_swa_k, cache_swa_v, norm_mix, norm_mlp, norm_final, ab_w_in, ab_w_s, ab_b_s, ab_ln_g, ab_ln_b, ab_w_o, swa_w_qkv, swa_b_qkv, swa_sinks, swa_w_o, swa_b_o, mlp_w_up, mlp_w_down):
    batch, seq, _ = x_prompt.shape
    dec_batch, dec_len, _ = x_sample.shape
    assert seq % MIXER_ROWS == 0 and seq % RET_CHUNK == 0
    kv_lanes = SWA_KV_HEADS * SWA_HD

    hp = x_prompt.reshape(batch * seq, D_MODEL)
    hs = x_sample.reshape(dec_batch * dec_len, D_MODEL)
    row = lambda v: v.reshape(1, -1)
    gfin = row(norm_final)

    w_in, w_o = ab_w_in[0].astype(BF16), ab_w_o[0].astype(BF16)
    hp, ret_p = _ab_prompt(hp, row(norm_mix[0]), w_in, w_o, ab_w_s[0], ab_b_s[0], ab_ln_g[0], ab_ln_b[0],
                           batch, seq)
    hs, ret_s, gm_s = _ab_sample(hs, row(norm_mix[0]), w_in, w_o, ab_w_s[0], ab_b_s[0], ab_ln_g[0], ab_ln_b[0],
                                 state_ret[0], dec_batch, dec_len)
    w_up, w_dn = mlp_w_up[0].astype(BF16), mlp_w_down[0].astype(BF16)
    hp, hs = _mlp(hp, hs, row(norm_mlp[0]), w_up, w_dn, gfin, final_norm=False, name="mlp0")

    w_qkv, w_so = swa_w_qkv[0].astype(BF16), swa_w_o[0].astype(BF16)
    hp, kp, vp = _swa_prompt(hp, row(norm_mix[1]), w_qkv, row(swa_b_qkv[0]), swa_sinks[0], w_so, row(swa_b_o[0]),
                             batch, seq)
    hs, ks, vs = _swa_sample(hs, row(norm_mix[1]), w_qkv, row(swa_b_qkv[0]), swa_sinks[0], w_so, row(swa_b_o[0]),
                             cache_swa_k[0].reshape(dec_batch, WINDOW, kv_lanes),
                             cache_swa_v[0].reshape(dec_batch, WINDOW, kv_lanes), dec_batch, dec_len)
    w_up, w_dn = mlp_w_up[1].astype(BF16), mlp_w_down[1].astype(BF16)
    hp, hs = _mlp(hp, hs, row(norm_mlp[1]), w_up, w_dn, gfin, final_norm=True, name="mlp1")

    cache_shape = lambda n: (1, n, WINDOW, SWA_KV_HEADS, SWA_HD)
    return (hp.reshape(batch, seq, D_MODEL),
            hs.reshape(dec_batch, dec_len, D_MODEL),
            ret_p[None],
            ret_s[None],
            gm_s.reshape(1, dec_batch, dec_len, GM_GROUPS * GM_GROUP_DIM),
            kp.reshape(cache_shape(batch)),
            vp.reshape(cache_shape(batch)),
            ks.reshape(cache_shape(dec_batch)),
            vs.reshape(cache_shape(dec_batch)))
```

```python
import functools

import numpy as np
import jax
import jax.numpy as jnp
from jax import lax
from jax.experimental import pallas as pl
from jax.experimental.pallas import tpu as pltpu

F32 = jnp.float32
BF16 = jnp.bfloat16

D_MODEL = 1024
PAST_LEN = 16384
RMS_EPS = 1e-6
LN_EPS = 1e-5
RET_HEADS = 4
RET_DK = 128
RET_DV = 128
RET_CHUNK = 128
ROPE_BASE = 10000.0
GM_GROUPS = 4
GM_GROUP_DIM = 128
GM_CHUNK = 128
AB_IN = 3072
AB_OUT = 1024
SWA_HEADS = 16
SWA_KV_HEADS = 2
SWA_HD = 64
WINDOW = 128
SWA_IN = 1280
SWA_OUT = 1024
D_FF = 4096

_Q0, _K0, _V0, _G0, _U0, _GV0 = 0, 512, 1024, 1536, 2048, 2560
_SK0, _SV0 = 1024, 1152

LANES = 128
MIXER_ROWS = 1024
MIXER_SPLIT = 4
MLP_ROWS = 1024
MLP_SPLIT = 4
FF_CHUNK = 1024
SAMPLE_BB = 16
VMEM_LIMIT = 56 * 1024 * 1024


def _rms(x, g):
    return x * lax.rsqrt(jnp.mean(x * x, axis=-1, keepdims=True) + RMS_EPS) * g


def _gelu(x):
    return 0.5 * x * (1.0 + lax.erf(x * np.float32(np.sqrt(0.5))))


def _silu(x):
    return x / (1.0 + jnp.exp(-x))


def _dot(a, b):
    return jnp.dot(a, b, preferred_element_type=F32)


def _dot_nt(a, b):
    return lax.dot_general(a, b, (((1,), (1,)), ((), ())), preferred_element_type=F32)


def _dot_tn(a, b):
    return lax.dot_general(a, b, (((0,), (0,)), ((), ())), preferred_element_type=F32)


def _const_spec(shape):
    zeros = (0,) * len(shape)
    return pl.BlockSpec(shape, lambda *_: zeros, pipeline_mode=pl.Buffered(1))


def _params(*sem):
    return pltpu.CompilerParams(dimension_semantics=sem, vmem_limit_bytes=VMEM_LIMIT)


def _mlp_kernel(x_ref, g_ref, wup_ref, wdn_ref, gf_ref, o_ref, *, final_norm):
    rows = x_ref.shape[0] // MLP_SPLIT
    groups = [slice(i * rows, (i + 1) * rows) for i in range(MLP_SPLIT)]
    xs = [x_ref[r, :] for r in groups]
    hns = [_rms(x, g_ref[...]).astype(BF16) for x in xs]
    accs = [jnp.zeros_like(x) for x in xs]
    for c in range(D_FF // FF_CHUNK):
        cols = slice(c * FF_CHUNK, (c + 1) * FF_CHUNK)
        for i in range(MLP_SPLIT):
            h = _dot(hns[i], wup_ref[:, cols])
            h = jnp.square(jnp.maximum(h, 0.0)).astype(BF16)
            accs[i] = accs[i] + _dot(h, wdn_ref[cols, :])
    for i, r in enumerate(groups):
        y = xs[i] + accs[i]
        if final_norm:
            y = _rms(y, gf_ref[...])
        o_ref[r, :] = y


def _mlp(x, g, wup, wdn, gf, *, final_norm, name):
    rows = x.shape[0]
    tm = min(MLP_ROWS, rows)
    row_spec = pl.BlockSpec((tm, D_MODEL), lambda i: (i, 0))
    return pl.pallas_call(
        functools.partial(_mlp_kernel, final_norm=final_norm),
        grid=(rows // tm,),
        in_specs=[
            row_spec,
            _const_spec((1, D_MODEL)),
            _const_spec((D_MODEL, D_FF)),
            _const_spec((D_FF, D_MODEL)),
            _const_spec((1, D_MODEL)),
        ],
        out_specs=row_spec,
        out_shape=jax.ShapeDtypeStruct((rows, D_MODEL), F32),
        compiler_params=_params("arbitrary"),
        name=name,
    )(x, g, wup, wdn, gf)


def _rotary(x, cos2, sin2):
    return x * cos2 + pltpu.roll(x, 64, 1) * sin2


def _retention_tables(chunk):
    log_g = jnp.log1p(-jnp.exp2(-5.0 - jnp.arange(RET_HEADS, dtype=F32)))
    idx = jnp.arange(chunk, dtype=F32)
    diff = idx[:, None] - idx[None, :]
    decay = jnp.where(diff[None] >= 0, jnp.exp(log_g[:, None, None] * jnp.maximum(diff, 0.0)[None]), 0.0)
    q_dec = jnp.exp(log_g[:, None] * (idx[None, :] + 1.0))
    k_dec = jnp.exp(log_g[:, None] * (chunk - 1.0 - idx[None, :]))
    log_g64 = np.log1p(-np.exp2(-5.0 - np.arange(RET_HEADS, dtype=np.float64)))
    chunk_dec = tuple(float(v) for v in np.exp(log_g64 * chunk))
    return decay, q_dec, k_dec, chunk_dec


def _rotary_tables(pos):
    half = RET_DK // 2
    inv = ROPE_BASE ** (-jnp.arange(half, dtype=F32) / half)
    ang = pos.astype(F32)[:, None] * inv[None, :]
    cos, sin = jnp.cos(ang), jnp.sin(ang)
    cos2 = jnp.concatenate([cos, cos], axis=-1)
    sin2 = jnp.concatenate([-sin, sin], axis=-1)
    scale = RET_DK ** -0.5
    return cos2 * scale, sin2 * scale, cos2, sin2


def _gmlp_branch(z_ref, rows, lng_ref, lnb_ref, gi):
    u = _gelu(z_ref[rows, _U0 + gi * LANES:_U0 + (gi + 1) * LANES])
    gv = _gelu(z_ref[rows, _GV0 + gi * LANES:_GV0 + (gi + 1) * LANES])
    mu = jnp.mean(gv, axis=-1, keepdims=True)
    cen = gv - mu
    var = jnp.mean(cen * cen, axis=-1, keepdims=True)
    gvn = cen * lax.rsqrt(var + LN_EPS) * lng_ref[gi:gi + 1, :] + lnb_ref[gi:gi + 1, :]
    return u, gvn


def _ab_prompt_kernel(x_ref, g_ref, win_ref, wo_ref, cq_ref, sq_ref, ck_ref, sk_ref,
                      dec_ref, qd_ref, kd_ref, ws_ref, bs_ref, lng_ref, lnb_ref,
                      o_ref, st_ref, *scratch, chunk_dec):
    si = pl.program_id(1)
    z_refs, cat_refs, s_ref = scratch[:MIXER_SPLIT], scratch[MIXER_SPLIT:2 * MIXER_SPLIT], scratch[-1]
    sub_rows = MIXER_ROWS // MIXER_SPLIT
    nchunk = sub_rows // RET_CHUNK

    @pl.when(si == 0)
    def _():
        s_ref[...] = jnp.zeros_like(s_ref)

    ri = lax.broadcasted_iota(jnp.int32, (GM_CHUNK, GM_CHUNK), 0)
    ci = lax.broadcasted_iota(jnp.int32, (GM_CHUNK, GM_CHUNK), 1)
    w_spatial = [jnp.where(ci <= ri, ws_ref[gi], 0.0).astype(BF16) for gi in range(GM_GROUPS)]
    tile_rows = lambda t: jnp.concatenate([t] * nchunk, axis=0)
    states = [s_ref[h] for h in range(RET_HEADS)]

    for sub in range(MIXER_SPLIT):
        rows = slice(sub * sub_rows, (sub + 1) * sub_rows)
        chunks = [slice(c * RET_CHUNK, (c + 1) * RET_CHUNK) for c in range(nchunk)]
        z_ref, cat_ref = z_refs[sub], cat_refs[sub]
        x = x_ref[rows, :]
        hn = _rms(x, g_ref[...]).astype(BF16)
        z_ref[:, :_U0] = _dot(hn, win_ref[:, :_U0])
        z_ref[:, _U0:] = _dot(hn, win_ref[:, _U0:])

        cq, sq, ck, sk = cq_ref[rows, :], sq_ref[rows, :], ck_ref[rows, :], sk_ref[rows, :]
        stage1 = []
        for h in range(RET_HEADS):
            q = _rotary(z_ref[:, _Q0 + h * LANES:_Q0 + (h + 1) * LANES], cq, sq)
            k = _rotary(z_ref[:, _K0 + h * LANES:_K0 + (h + 1) * LANES], ck, sk)
            v = z_ref[:, _V0 + h * LANES:_V0 + (h + 1) * LANES].astype(BF16)
            qb, kb = q.astype(BF16), k.astype(BF16)
            qdb = (q * tile_rows(qd_ref[h])).astype(BF16)
            kdb = (k * tile_rows(kd_ref[h])).astype(BF16)
            local = [slice(c * RET_CHUNK, (c + 1) * RET_CHUNK) for c in range(nchunk)]
            scores = [_dot_nt(qb[r], kb[r]) for r in local]
            kvs = [_dot_tn(kdb[r], v[r]) for r in local]
            before = []
            for kv in kvs:
                before.append(states[h].astype(BF16))
                states[h] = states[h] * chunk_dec[h] + kv
            stage1.append((v, qdb, scores, before, local))
        for h, (v, qdb, scores, before, local) in enumerate(stage1):
            lanes = slice(h * LANES, (h + 1) * LANES)
            dec = dec_ref[h]
            for c, r in enumerate(local):
                lhs = jnp.concatenate([(scores[c] * dec).astype(BF16), qdb[r]], axis=1)
                rhs = jnp.concatenate([v[r], before[c]], axis=0)
                o = _dot(lhs, rhs)
                o = o * lax.rsqrt(jnp.mean(o * o, axis=-1, keepdims=True) + RMS_EPS)
                gate = _silu(z_ref[chunks[c], _G0 + h * LANES:_G0 + (h + 1) * LANES])
                cat_ref[chunks[c], lanes] = (gate * o).astype(BF16)

        for gi in range(GM_GROUPS):
            u, gvn = _gmlp_branch(z_ref, slice(None), lng_ref, lnb_ref, gi)
            gvb = gvn.astype(BF16)
            rhs = jnp.concatenate([gvb[c * GM_CHUNK:(c + 1) * GM_CHUNK] for c in range(nchunk)], axis=1)
            mixed = _dot(w_spatial[gi], rhs)
            bias = bs_ref[gi]
            for c in range(nchunk):
                local_rows = slice(c * GM_CHUNK, (c + 1) * GM_CHUNK)
                m = mixed[:, c * GM_CHUNK:(c + 1) * GM_CHUNK] + bias
                cat_ref[chunks[c], 512 + gi * LANES:512 + (gi + 1) * LANES] = (u[local_rows] * m).astype(BF16)

        o_ref[rows, :] = x + _dot(cat_ref[...], wo_ref[...])

    for h in range(RET_HEADS):
        s_ref[h] = states[h]

    @pl.when(si == pl.num_programs(1) - 1)
    def _():
        st_ref[...] = s_ref[...]


def _ab_prompt(x, g, w_in, w_o, w_s, b_s, ln_g, ln_b, batch, seq):
    nstep = seq // MIXER_ROWS
    decay, q_dec, k_dec, chunk_dec = _retention_tables(RET_CHUNK)
    cq, sq, ck, sk = _rotary_tables(jnp.arange(seq, dtype=jnp.int32))
    lane_b = lambda t: jnp.broadcast_to(t[:, :, None], t.shape + (LANES,))
    row_spec = pl.BlockSpec((MIXER_ROWS, D_MODEL), lambda b, s: (b * nstep + s, 0))
    tab_spec = pl.BlockSpec((MIXER_ROWS, LANES), lambda b, s: (s, 0))
    cube = (RET_HEADS, RET_CHUNK, LANES)
    out, state = pl.pallas_call(
        functools.partial(_ab_prompt_kernel, chunk_dec=chunk_dec),
        grid=(batch, nstep),
        in_specs=[
            row_spec,
            _const_spec((1, D_MODEL)),
            _const_spec((D_MODEL, AB_IN)),
            _const_spec((AB_OUT, D_MODEL)),
            tab_spec, tab_spec, tab_spec, tab_spec,
            _const_spec(cube), _const_spec(cube), _const_spec(cube),
            _const_spec(cube), _const_spec(cube),
            _const_spec((GM_GROUPS, GM_GROUP_DIM)), _const_spec((GM_GROUPS, GM_GROUP_DIM)),
        ],
        out_specs=[
            row_spec,
            pl.BlockSpec((None, RET_HEADS, RET_DK, RET_DV), lambda b, s: (b, 0, 0, 0)),
        ],
        out_shape=[
            jax.ShapeDtypeStruct((batch * seq, D_MODEL), F32),
            jax.ShapeDtypeStruct((batch, RET_HEADS, RET_DK, RET_DV), F32),
        ],
        scratch_shapes=(
            [pltpu.VMEM((MIXER_ROWS // MIXER_SPLIT, AB_IN), F32)] * MIXER_SPLIT
            + [pltpu.VMEM((MIXER_ROWS // MIXER_SPLIT, AB_OUT), BF16)] * MIXER_SPLIT
            + [pltpu.VMEM((RET_HEADS, RET_DK, RET_DV), F32)]),
        compiler_params=_params("arbitrary", "arbitrary"),
        name="ab_prompt",
    )(x, g, w_in, w_o, cq, sq, ck, sk, decay, lane_b(q_dec), lane_b(k_dec),
      w_s, lane_b(b_s), ln_g, ln_b)
    return out, state


def _ab_sample_kernel(x_ref, g_ref, win_ref, wo_ref, cq_ref, sq_ref, ck_ref, sk_ref,
                      dec_ref, qd_ref, kd_ref, ws_ref, bs_ref, lng_ref, lnb_ref, causal_ref, st_ref,
                      o_ref, sto_ref, gvn_ref, z_ref, cat_ref, *, chunk_dec, dec_len):
    rows_n = SAMPLE_BB * dec_len
    x = x_ref[...]
    hn = _rms(x, g_ref[...]).astype(BF16)
    z_ref[...] = _dot(hn, win_ref[...])

    cq, sq, ck, sk = cq_ref[...], sq_ref[...], ck_ref[...], sk_ref[...]
    token = lax.broadcasted_iota(jnp.int32, (rows_n, rows_n), 1)
    for h in range(RET_HEADS):
        lanes = slice(h * LANES, (h + 1) * LANES)
        q = _rotary(z_ref[:, _Q0 + h * LANES:_Q0 + (h + 1) * LANES], cq, sq)
        k = _rotary(z_ref[:, _K0 + h * LANES:_K0 + (h + 1) * LANES], ck, sk)
        v = z_ref[:, _V0 + h * LANES:_V0 + (h + 1) * LANES].astype(BF16)
        scores = _dot_nt(q.astype(BF16), k.astype(BF16)) * dec_ref[h]
        intra = _dot(scores.astype(BF16), v)
        qs = q * qd_ref[h]
        kst = (k * kd_ref[h]).T
        kst_all = jnp.concatenate(
            [jnp.where((token >= b * dec_len) & (token < (b + 1) * dec_len), kst, 0.0).astype(BF16)
             for b in range(SAMPLE_BB)], axis=0)
        kv_all = _dot(kst_all, v)
        cross = []
        for b in range(SAMPLE_BB):
            rows = slice(b * dec_len, (b + 1) * dec_len)
            state = st_ref[b, h]
            cross.append(_dot(qs[rows].astype(BF16), state.astype(BF16)))
            sto_ref[b, h] = state * chunk_dec[h] + kv_all[b * RET_DK:(b + 1) * RET_DK]
        o = intra + jnp.concatenate(cross, axis=0)
        o = o * lax.rsqrt(jnp.mean(o * o, axis=-1, keepdims=True) + RMS_EPS)
        gate = _silu(z_ref[:, _G0 + h * LANES:_G0 + (h + 1) * LANES])
        cat_ref[:, lanes] = (gate * o).astype(BF16)

    causal = causal_ref[...] > 0.0
    for gi in range(GM_GROUPS):
        u, gvn = _gmlp_branch(z_ref, slice(None), lng_ref, lnb_ref, gi)
        gvn_ref[:, gi * LANES:(gi + 1) * LANES] = gvn
        w = jnp.where(causal, ws_ref[gi], 0.0).astype(BF16)
        mixed = _dot(w, gvn.astype(BF16)) + bs_ref[gi]
        cat_ref[:, 512 + gi * LANES:512 + (gi + 1) * LANES] = (u * mixed).astype(BF16)

    o_ref[...] = x + _dot(cat_ref[...], wo_ref[...])


def _ab_sample(x, g, w_in, w_o, w_s, b_s, ln_g, ln_b, state, dec_batch, dec_len):
    rows_n = SAMPLE_BB * dec_len
    assert rows_n == LANES and dec_batch % SAMPLE_BB == 0
    decay, q_dec, k_dec, chunk_dec = _retention_tables(dec_len)
    cq, sq, ck, sk = (jnp.tile(t, (SAMPLE_BB, 1)) for t in
                      _rotary_tables(PAST_LEN + jnp.arange(dec_len, dtype=jnp.int32)))
    eye = jnp.eye(SAMPLE_BB, dtype=F32)
    bd_decay = jax.vmap(lambda d: jnp.kron(eye, d))(decay)
    rows_b = lambda t: jnp.broadcast_to(jnp.tile(t, (1, SAMPLE_BB))[:, :, None], (t.shape[0], rows_n, LANES))
    w_tiled = jnp.tile(w_s[:, :dec_len, :dec_len], (1, SAMPLE_BB, SAMPLE_BB))
    causal = jnp.asarray(np.kron(np.eye(SAMPLE_BB), np.tril(np.ones((dec_len, dec_len)))), F32)
    row_spec = pl.BlockSpec((rows_n, D_MODEL), lambda i: (i, 0))
    st_spec = pl.BlockSpec((SAMPLE_BB, RET_HEADS, RET_DK, RET_DV), lambda i: (i, 0, 0, 0))
    cube = (RET_HEADS, rows_n, LANES)
    tab = (rows_n, LANES)
    return pl.pallas_call(
        functools.partial(_ab_sample_kernel, chunk_dec=chunk_dec, dec_len=dec_len),
        grid=(dec_batch // SAMPLE_BB,),
        in_specs=[
            row_spec,
            _const_spec((1, D_MODEL)),
            _const_spec((D_MODEL, AB_IN)),
            _const_spec((AB_OUT, D_MODEL)),
            _const_spec(tab), _const_spec(tab), _const_spec(tab), _const_spec(tab),
            _const_spec(cube), _const_spec(cube), _const_spec(cube),
            _const_spec(cube), _const_spec(cube),
            _const_spec((GM_GROUPS, GM_GROUP_DIM)), _const_spec((GM_GROUPS, GM_GROUP_DIM)),
            _const_spec(tab),
            st_spec,
        ],
        out_specs=[
            row_spec,
            st_spec,
            pl.BlockSpec((rows_n, GM_GROUPS * GM_GROUP_DIM), lambda i: (i, 0)),
        ],
        out_shape=[
            jax.ShapeDtypeStruct((dec_batch * dec_len, D_MODEL), F32),
            jax.ShapeDtypeStruct(state.shape, F32),
            jax.ShapeDtypeStruct((dec_batch * dec_len, GM_GROUPS * GM_GROUP_DIM), F32),
        ],
        scratch_shapes=[
            pltpu.VMEM((rows_n, AB_IN), F32),
            pltpu.VMEM((rows_n, AB_OUT), BF16),
        ],
        compiler_params=_params("arbitrary"),
        name="ab_sample",
    )(x, g, w_in, w_o, cq, sq, ck, sk, bd_decay, rows_b(q_dec), rows_b(k_dec),
      w_tiled, rows_b(b_s[:, :dec_len]), ln_g, ln_b, causal, state)


_LOG2E = float(np.log2(np.e))


def _softmax_numerators(s, sink_even, sink_odd):
    s0, s1, s2, s3 = (s[:, i * LANES:(i + 1) * LANES] for i in range(4))
    m_even = jnp.max(jnp.maximum(jnp.maximum(s0, s1), sink_even), axis=-1, keepdims=True)
    m_odd = jnp.max(jnp.maximum(jnp.maximum(s2, s3), sink_odd), axis=-1, keepdims=True)
    p = jnp.concatenate([jnp.exp2(s0 - m_even), jnp.exp2(s1 - m_even),
                         jnp.exp2(s2 - m_odd), jnp.exp2(s3 - m_odd)], axis=1)
    even_lane = lax.broadcasted_iota(jnp.int32, sink_even.shape, 1) < SWA_HD
    sink_term = jnp.where(even_lane, jnp.exp2(sink_even - m_even), jnp.exp2(sink_odd - m_odd))
    return p.astype(BF16), sink_term


def _band_bias(q_rows, first_key_row):
    i = np.arange(q_rows)[:, None]
    j = np.arange(2 * WINDOW)[None, :]
    rel = i + first_key_row - j
    half = np.where((rel >= 0) & (rel <= WINDOW), 0.0, -np.inf).astype(np.float32)
    return np.concatenate([half, half], axis=1)


def _kv_layouts(kcol, vcol):
    even_lane = lax.broadcasted_iota(jnp.int32, kcol.shape, 1) < SWA_HD
    ks = kcol * (SWA_HD ** -0.5 * _LOG2E)
    kr = pltpu.roll(ks, SWA_HD, 1)
    vr = pltpu.roll(vcol, SWA_HD, 1)
    one_even = jnp.where(even_lane, 1.0, 0.0)
    one_odd = 1.0 - one_even
    zero = jnp.zeros_like(kcol)
    out = []
    for kvh in range(SWA_KV_HEADS):
        k_lo, k_hi = (ks, kr) if kvh == 0 else (kr, ks)
        v_lo, v_hi = (vcol, vr) if kvh == 0 else (vr, vcol)
        ktop = jnp.where(even_lane, k_lo, zero).astype(BF16)
        kbot = jnp.where(even_lane, zero, k_hi).astype(BF16)
        vtop = jnp.concatenate([jnp.where(even_lane, v_lo, zero), one_even], axis=1).astype(BF16)
        vbot = jnp.concatenate([jnp.where(even_lane, zero, v_hi), one_odd], axis=1).astype(BF16)
        out.append((ktop, kbot, vtop, vbot))
    return out


def _swa_prompt_kernel(x_ref, xh_ref, g_ref, wqkv_ref, bqkv_ref, wo_ref, bo_ref, sink_ref, bias_ref,
                       o_ref, kc_ref, vc_ref, z_ref, cat_ref):
    si = pl.program_id(1)
    nblk = MIXER_ROWS // WINDOW
    npair = SWA_HEADS // SWA_KV_HEADS // 2
    sub_rows = MIXER_ROWS // MIXER_SPLIT

    x = x_ref[...]
    hn = _rms(x, g_ref[...]).astype(BF16)
    for sub in range(MIXER_SPLIT):
        rows = slice(sub * sub_rows, (sub + 1) * sub_rows)
        z_ref[rows, :] = _dot(hn[rows], wqkv_ref[...]) + bqkv_ref[...]
    hh = _rms(xh_ref[...], g_ref[...]).astype(BF16)
    zh = _dot(hh, wqkv_ref[:, _SK0:]) + bqkv_ref[:, _SK0:]

    kcol = z_ref[:, _SK0:_SK0 + LANES]
    vcol = z_ref[:, _SV0:_SV0 + LANES]
    kc_ref[...] = kcol[MIXER_ROWS - WINDOW:]
    vc_ref[...] = vcol[MIXER_ROWS - WINDOW:]
    layouts = _kv_layouts(jnp.concatenate([zh[:, :LANES], kcol], axis=0),
                          jnp.concatenate([zh[:, LANES:], vcol], axis=0))

    for r in range(nblk):
        rows = slice(r * WINDOW, (r + 1) * WINDOW)
        keys = slice(r * WINDOW, (r + 2) * WINDOW)
        tile = bias_ref[jnp.where(si > 0, 0, 1)] if r == 0 else bias_ref[0]
        bias = jnp.concatenate([tile] * npair, axis=0)
        for kvh, (ktop, kbot, vtop, vbot) in enumerate(layouts):
            c0 = kvh * npair * LANES
            lhs = jnp.concatenate(
                [z_ref[rows, c0 + p * LANES:c0 + (p + 1) * LANES] for p in range(npair)], axis=0).astype(BF16)
            s = _dot_nt(lhs, jnp.concatenate([ktop[keys], kbot[keys]], axis=0)) + bias
            p, sink_term = _softmax_numerators(s, sink_ref[kvh, 0], sink_ref[kvh, 1])
            out = _dot(p, jnp.concatenate([vtop[keys], vbot[keys]], axis=0))
            o = out[:, :LANES] / (out[:, LANES:] + sink_term)
            for p_i in range(npair):
                cat_ref[rows, c0 + p_i * LANES:c0 + (p_i + 1) * LANES] = (
                    o[p_i * WINDOW:(p_i + 1) * WINDOW].astype(BF16))

    o_ref[...] = x + _dot(cat_ref[...], wo_ref[...]) + bo_ref[...]


def _sink_table(sinks, rows_per_pair):
    npair = SWA_HEADS // SWA_KV_HEADS // 2
    s = (sinks.astype(F32) * _LOG2E).reshape(SWA_KV_HEADS, npair, 2).transpose(0, 2, 1)[:, :, :, None, None]
    s = jnp.broadcast_to(s, (SWA_KV_HEADS, 2, npair, rows_per_pair, LANES))
    return s.reshape(SWA_KV_HEADS, 2, npair * rows_per_pair, LANES)


def _swa_prompt(x, g, w_qkv, b_qkv, sinks, w_o, b_o, batch, seq):
    nstep = seq // MIXER_ROWS
    npair = SWA_HEADS // SWA_KV_HEADS // 2
    row_spec = pl.BlockSpec((MIXER_ROWS, D_MODEL), lambda b, s: (b * nstep + s, 0))
    nblk = MIXER_ROWS // WINDOW
    halo_spec = pl.BlockSpec((WINDOW, D_MODEL), lambda b, s: (jnp.maximum((b * nstep + s) * nblk - 1, 0), 0))
    cache_spec = pl.BlockSpec((None, WINDOW, LANES), lambda b, s: (b, 0, 0))
    band = _band_bias(WINDOW, WINDOW)
    no_prev = np.where(np.arange(4 * WINDOW)[None, :] % (2 * WINDOW) >= WINDOW, band, -np.inf).astype(np.float32)
    bias = jnp.asarray(np.stack([band, no_prev]))
    return pl.pallas_call(
        _swa_prompt_kernel,
        grid=(batch, nstep),
        in_specs=[
            row_spec,
            halo_spec,
            _const_spec((1, D_MODEL)),
            _const_spec((D_MODEL, SWA_IN)),
            _const_spec((1, SWA_IN)),
            _const_spec((SWA_OUT, D_MODEL)),
            _const_spec((1, D_MODEL)),
            _const_spec((SWA_KV_HEADS, 2, npair * WINDOW, LANES)),
            _const_spec((2, WINDOW, 4 * WINDOW)),
        ],
        out_specs=[row_spec, cache_spec, cache_spec],
        out_shape=[
            jax.ShapeDtypeStruct((batch * seq, D_MODEL), F32),
            jax.ShapeDtypeStruct((batch, WINDOW, LANES), F32),
            jax.ShapeDtypeStruct((batch, WINDOW, LANES), F32),
        ],
        scratch_shapes=[
            pltpu.VMEM((MIXER_ROWS, SWA_IN), F32),
            pltpu.VMEM((MIXER_ROWS, SWA_OUT), BF16),
        ],
        compiler_params=_params("arbitrary", "arbitrary"),
        name="swa_prompt",
    )(x, x, g, w_qkv, b_qkv, w_o, b_o, _sink_table(sinks, WINDOW), bias)


def _swa_sample_kernel(x_ref, g_ref, wqkv_ref, bqkv_ref, wo_ref, bo_ref, sink_ref, bias_ref, ck_ref, cv_ref,
                       o_ref, nk_ref, nv_ref, z_ref, cat_ref, *, dec_len):
    ncol = SWA_HEADS // 2
    grp = ncol * dec_len
    x = x_ref[...]
    hn = _rms(x, g_ref[...]).astype(BF16)
    z_ref[...] = _dot(hn, wqkv_ref[...]) + bqkv_ref[...]

    bias, sink = bias_ref[...], sink_ref[...]
    lane_lo = lax.broadcasted_iota(jnp.int32, (grp, LANES), 1) < SWA_HD
    kv0 = lax.broadcasted_iota(jnp.int32, (grp, LANES), 0) < grp // 2
    own_half = jnp.where(kv0, 0, 1) == jnp.where(lane_lo, 0, 1)
    pad = jnp.zeros((WINDOW - dec_len, LANES), F32)
    elems = [slice(b * dec_len, (b + 1) * dec_len) for b in range(SAMPLE_BB)]
    values, scores = [], []
    for b, rows in enumerate(elems):
        knew = z_ref[rows, _SK0:_SK0 + LANES]
        vnew = z_ref[rows, _SV0:_SV0 + LANES]
        kold, vold = ck_ref[b], cv_ref[b]
        nk_ref[b, 0:WINDOW - dec_len, :] = kold[dec_len:]
        nk_ref[b, WINDOW - dec_len:, :] = knew
        nv_ref[b, 0:WINDOW - dec_len, :] = vold[dec_len:]
        nv_ref[b, WINDOW - dec_len:, :] = vnew
        kall = jnp.concatenate([kold, knew, pad], axis=0).astype(BF16)
        values.append(jnp.concatenate([vold, vnew, pad], axis=0).astype(BF16))

        q = jnp.concatenate([z_ref[rows, j * LANES:(j + 1) * LANES] for j in range(ncol)], axis=0)
        q = q * (SWA_HD ** -0.5 * _LOG2E)
        q_swapped = pltpu.roll(q, SWA_HD, 1)
        q_even = jnp.where(own_half, jnp.where(kv0, q, q_swapped), 0.0)
        q_odd = jnp.where(own_half, jnp.where(kv0, q_swapped, q), 0.0)
        lhs = jnp.concatenate([q_even, q_odd], axis=0).astype(BF16)
        scores.append(_dot_nt(lhs, kall) + bias)

    probs, dens = [], []
    for s in scores:
        m = jnp.max(jnp.maximum(jnp.maximum(s[:, :LANES], s[:, LANES:]), sink), axis=-1, keepdims=True)
        p = jnp.exp2(s - m)
        dens.append(jnp.sum(p, axis=-1, keepdims=True) + jnp.exp2(sink - m))
        probs.append(p.astype(BF16))

    outs = [_dot(p, v) for p, v in zip(probs, values)]

    for rows, out, den in zip(elems, outs, dens):
        o = out / den
        o_swapped = pltpu.roll(o, SWA_HD, 1)
        cols = jnp.where(lane_lo,
                         jnp.where(kv0, o[:grp], o_swapped[:grp]),
                         jnp.where(kv0, o_swapped[grp:], o[grp:]))
        for j in range(ncol):
            cat_ref[rows, j * LANES:(j + 1) * LANES] = cols[j * dec_len:(j + 1) * dec_len]

    o_ref[...] = x + _dot(cat_ref[...].astype(BF16), wo_ref[...]) + bo_ref[...]


def _swa_sample(x, g, w_qkv, b_qkv, sinks, w_o, b_o, cache_k, cache_v, dec_batch, dec_len):
    rows_n = SAMPLE_BB * dec_len
    ncol = SWA_HEADS // 2
    score_rows = 2 * ncol * dec_len
    sink = jnp.broadcast_to((sinks.astype(F32) * _LOG2E).reshape(ncol, 2).T[:, :, None, None],
                            (2, ncol, dec_len, LANES)).reshape(score_rows, LANES)
    bias = jnp.asarray(np.tile(_band_bias(dec_len, WINDOW)[:, :2 * WINDOW], (2 * ncol, 1)))
    row_spec = pl.BlockSpec((rows_n, D_MODEL), lambda i: (i, 0))
    cache_spec = pl.BlockSpec((SAMPLE_BB, WINDOW, LANES), lambda i: (i, 0, 0))
    return pl.pallas_call(
        functools.partial(_swa_sample_kernel, dec_len=dec_len),
        grid=(dec_batch // SAMPLE_BB,),
        in_specs=[
            row_spec,
            _const_spec((1, D_MODEL)),
            _const_spec((D_MODEL, SWA_IN)),
            _const_spec((1, SWA_IN)),
            _const_spec((SWA_OUT, D_MODEL)),
            _const_spec((1, D_MODEL)),
            _const_spec((score_rows, LANES)),
            _const_spec((score_rows, 2 * WINDOW)),
            cache_spec, cache_spec,
        ],
        out_specs=[row_spec, cache_spec, cache_spec],
        out_shape=[
            jax.ShapeDtypeStruct((dec_batch * dec_len, D_MODEL), F32),
            jax.ShapeDtypeStruct((dec_batch, WINDOW, LANES), F32),
            jax.ShapeDtypeStruct((dec_batch, WINDOW, LANES), F32),
        ],
        scratch_shapes=[
            pltpu.VMEM((rows_n, SWA_IN), F32),
            pltpu.VMEM((rows_n, SWA_OUT), F32),
        ],
        compiler_params=_params("arbitrary"),
        name="swa_sample",
    )(x, g, w_qkv, b_qkv, w_o, b_o, sink, bias, cache_k, cache_v)


def kernel(x_prompt, x_sample, state_ret, cache_swa_k, cache_swa_v, norm_mix, norm_mlp, norm_final, ab_w_in, ab_w_s, ab_b_s, ab_ln_g, ab_ln_b, ab_w_o, swa_w_qkv, swa_b_qkv, swa_sinks, swa_w_o, swa_b_o, mlp_w_up, mlp_w_down):
    batch, seq, _ = x_prompt.shape
    dec_batch, dec_len, _ = x_sample.shape
    assert seq % MIXER_ROWS == 0 and seq % RET_CHUNK == 0
    kv_lanes = SWA_KV_HEADS * SWA_HD

    hp = x_prompt.reshape(batch * seq, D_MODEL)
    hs = x_sample.reshape(dec_batch * dec_len, D_MODEL)
    row = lambda v: v.reshape(1, -1)
    gfin = row(norm_final)

    w_in, w_o = ab_w_in[0].astype(BF16), ab_w_o[0].astype(BF16)
    hp, ret_p = _ab_prompt(hp, row(norm_mix[0]), w_in, w_o, ab_w_s[0], ab_b_s[0], ab_ln_g[0], ab_ln_b[0],
                           batch, seq)
    hs, ret_s, gm_s = _ab_sample(hs, row(norm_mix[0]), w_in, w_o, ab_w_s[0], ab_b_s[0], ab_ln_g[0], ab_ln_b[0],
                                 state_ret[0], dec_batch, dec_len)
    w_up, w_dn = mlp_w_up[0].astype(BF16), mlp_w_down[0].astype(BF16)
    hp = _mlp(hp, row(norm_mlp[0]), w_up, w_dn, gfin, final_norm=False, name="mlp0_prompt")
    hs = _mlp(hs, row(norm_mlp[0]), w_up, w_dn, gfin, final_norm=False, name="mlp0_sample")

    w_qkv, w_so = swa_w_qkv[0].astype(BF16), swa_w_o[0].astype(BF16)
    hp, kp, vp = _swa_prompt(hp, row(norm_mix[1]), w_qkv, row(swa_b_qkv[0]), swa_sinks[0], w_so, row(swa_b_o[0]),
                             batch, seq)
    hs, ks, vs = _swa_sample(hs, row(norm_mix[1]), w_qkv, row(swa_b_qkv[0]), swa_sinks[0], w_so, row(swa_b_o[0]),
                             cache_swa_k[0].reshape(dec_batch, WINDOW, kv_lanes),
                             cache_swa_v[0].reshape(dec_batch, WINDOW, kv_lanes), dec_batch, dec_len)
    w_up, w_dn = mlp_w_up[1].astype(BF16), mlp_w_down[1].astype(BF16)
    hp = _mlp(hp, row(norm_mlp[1]), w_up, w_dn, gfin, final_norm=True, name="mlp1_prompt")
    hs = _mlp(hs, row(norm_mlp[1]), w_up, w_dn, gfin, final_norm=True, name="mlp1_sample")

    cache_shape = lambda n: (1, n, WINDOW, SWA_KV_HEADS, SWA_HD)
    return (hp.reshape(batch, seq, D_MODEL),
            hs.reshape(dec_batch, dec_len, D_MODEL),
            ret_p[None],
            ret_s[None],
            gm_s.reshape(1, dec_batch, dec_len, GM_GROUPS * GM_GROUP_DIM),
            kp.reshape(cache_shape(batch)),
            vp.reshape(cache_shape(batch)),
            ks.reshape(cache_shape(dec_batch)),
            vs.reshape(cache_shape(dec_batch)))
```

```python
import functools

import numpy as np
import jax
import jax.numpy as jnp
from jax import lax
from jax.experimental import pallas as pl
from jax.experimental.pallas import tpu as pltpu

F32 = jnp.float32
BF16 = jnp.bfloat16

D_MODEL = 1024
PAST_LEN = 16384
RMS_EPS = 1e-6
LN_EPS = 1e-5
RET_HEADS = 4
RET_DK = 128
RET_DV = 128
RET_CHUNK = 128
ROPE_BASE = 10000.0
GM_GROUPS = 4
GM_GROUP_DIM = 128
GM_CHUNK = 128
AB_IN = 3072
AB_OUT = 1024
SWA_HEADS = 16
SWA_KV_HEADS = 2
SWA_HD = 64
WINDOW = 128
SWA_IN = 1280
SWA_OUT = 1024
D_FF = 4096

_Q0, _K0, _V0, _G0, _U0, _GV0 = 0, 512, 1024, 1536, 2048, 2560
_SK0, _SV0 = 1024, 1152

LANES = 128
MIXER_ROWS = 1024
MIXER_SPLIT = 4
MLP_ROWS = 1024
MLP_SPLIT = 4
FF_CHUNK = 1024
SAMPLE_BB = 16
VMEM_LIMIT = 56 * 1024 * 1024


def _rms(x, g):
    return x * lax.rsqrt(jnp.mean(x * x, axis=-1, keepdims=True) + RMS_EPS) * g


def _gelu(x):
    return 0.5 * x * (1.0 + lax.erf(x * np.float32(np.sqrt(0.5))))


def _silu(x):
    return x / (1.0 + jnp.exp(-x))


def _dot(a, b):
    return jnp.dot(a, b, preferred_element_type=F32)


def _dot_nt(a, b):
    return lax.dot_general(a, b, (((1,), (1,)), ((), ())), preferred_element_type=F32)


def _dot_tn(a, b):
    return lax.dot_general(a, b, (((0,), (0,)), ((), ())), preferred_element_type=F32)


def _const_spec(shape):
    zeros = (0,) * len(shape)
    return pl.BlockSpec(shape, lambda *_: zeros, pipeline_mode=pl.Buffered(1))


def _params(*sem):
    return pltpu.CompilerParams(dimension_semantics=sem, vmem_limit_bytes=VMEM_LIMIT)


def _mlp_kernel(x_ref, g_ref, wup_ref, wdn_ref, gf_ref, o_ref, *, final_norm):
    rows = x_ref.shape[0] // MLP_SPLIT
    groups = [slice(i * rows, (i + 1) * rows) for i in range(MLP_SPLIT)]
    xs = [x_ref[r, :] for r in groups]
    hns = [_rms(x, g_ref[...]).astype(BF16) for x in xs]
    accs = [jnp.zeros_like(x) for x in xs]
    for c in range(D_FF // FF_CHUNK):
        cols = slice(c * FF_CHUNK, (c + 1) * FF_CHUNK)
        for i in range(MLP_SPLIT):
            h = _dot(hns[i], wup_ref[:, cols])
            h = jnp.square(jnp.maximum(h, 0.0)).astype(BF16)
            accs[i] = accs[i] + _dot(h, wdn_ref[cols, :])
    for i, r in enumerate(groups):
        y = xs[i] + accs[i]
        if final_norm:
            y = _rms(y, gf_ref[...])
        o_ref[r, :] = y


def _mlp(x, g, wup, wdn, gf, *, final_norm, name):
    rows = x.shape[0]
    tm = min(MLP_ROWS, rows)
    row_spec = pl.BlockSpec((tm, D_MODEL), lambda i: (i, 0))
    return pl.pallas_call(
        functools.partial(_mlp_kernel, final_norm=final_norm),
        grid=(rows // tm,),
        in_specs=[
            row_spec,
            _const_spec((1, D_MODEL)),
            _const_spec((D_MODEL, D_FF)),
            _const_spec((D_FF, D_MODEL)),
            _const_spec((1, D_MODEL)),
        ],
        out_specs=row_spec,
        out_shape=jax.ShapeDtypeStruct((rows, D_MODEL), F32),
        compiler_params=_params("arbitrary"),
        name=name,
    )(x, g, wup, wdn, gf)


def _rotary(x, cos2, sin2):
    return x * cos2 + pltpu.roll(x, 64, 1) * sin2


def _retention_tables(chunk):
    log_g = np.log1p(-np.exp2(-5.0 - np.arange(RET_HEADS, dtype=np.float64)))
    idx = np.arange(chunk, dtype=np.float64)
    diff = idx[:, None] - idx[None, :]
    decay = np.where(diff[None] >= 0, np.exp(log_g[:, None, None] * np.maximum(diff, 0.0)[None]), 0.0)
    q_dec = np.exp(log_g[:, None] * (idx[None, :] + 1.0))
    k_dec = np.exp(log_g[:, None] * (chunk - 1.0 - idx[None, :]))
    chunk_dec = tuple(float(v) for v in np.exp(log_g * chunk))
    f32 = lambda t: t.astype(np.float32)
    return f32(decay), f32(q_dec), f32(k_dec), chunk_dec


def _rotary_tables(pos):
    half = RET_DK // 2
    inv = ROPE_BASE ** (-np.arange(half, dtype=np.float64) / half)
    ang = np.asarray(pos, np.float64)[:, None] * inv[None, :]
    cos, sin = np.cos(ang), np.sin(ang)
    cos2 = np.concatenate([cos, cos], axis=-1)
    sin2 = np.concatenate([-sin, sin], axis=-1)
    scale = RET_DK ** -0.5
    return np.concatenate([cos2 * scale, sin2 * scale, cos2, sin2], axis=-1).astype(np.float32)


def _gmlp_branch(z_ref, rows, lng_ref, lnb_ref, gi):
    u = _gelu(z_ref[rows, _U0 + gi * LANES:_U0 + (gi + 1) * LANES])
    gv = _gelu(z_ref[rows, _GV0 + gi * LANES:_GV0 + (gi + 1) * LANES])
    mu = jnp.mean(gv, axis=-1, keepdims=True)
    cen = gv - mu
    var = jnp.mean(cen * cen, axis=-1, keepdims=True)
    gvn = cen * lax.rsqrt(var + LN_EPS) * lng_ref[gi:gi + 1, :] + lnb_ref[gi:gi + 1, :]
    return u, gvn


def _ab_prompt_kernel(x_ref, g_ref, win_ref, wo_ref, rot_ref,
                      dec_ref, qd_ref, kd_ref, ws_ref, bs_ref, lng_ref, lnb_ref,
                      o_ref, st_ref, *scratch, chunk_dec):
    si = pl.program_id(1)
    z_refs, cat_refs, s_ref = scratch[:MIXER_SPLIT], scratch[MIXER_SPLIT:2 * MIXER_SPLIT], scratch[-1]
    sub_rows = MIXER_ROWS // MIXER_SPLIT
    nchunk = sub_rows // RET_CHUNK

    @pl.when(si == 0)
    def _():
        s_ref[...] = jnp.zeros_like(s_ref)

    ri = lax.broadcasted_iota(jnp.int32, (GM_CHUNK, GM_CHUNK), 0)
    ci = lax.broadcasted_iota(jnp.int32, (GM_CHUNK, GM_CHUNK), 1)
    w_spatial = [jnp.where(ci <= ri, ws_ref[gi], 0.0).astype(BF16) for gi in range(GM_GROUPS)]
    tile_rows = lambda t: jnp.concatenate([t] * nchunk, axis=0)
    states = [s_ref[h] for h in range(RET_HEADS)]

    for sub in range(MIXER_SPLIT):
        rows = slice(sub * sub_rows, (sub + 1) * sub_rows)
        chunks = [slice(c * RET_CHUNK, (c + 1) * RET_CHUNK) for c in range(nchunk)]
        z_ref, cat_ref = z_refs[sub], cat_refs[sub]
        x = x_ref[rows, :]
        hn = _rms(x, g_ref[...]).astype(BF16)
        z_ref[:, :_U0] = _dot(hn, win_ref[:, :_U0])
        z_ref[:, _U0:] = _dot(hn, win_ref[:, _U0:])

        cq, sq, ck, sk = (rot_ref[rows, i * LANES:(i + 1) * LANES] for i in range(4))
        stage1 = []
        for h in range(RET_HEADS):
            q = _rotary(z_ref[:, _Q0 + h * LANES:_Q0 + (h + 1) * LANES], cq, sq)
            k = _rotary(z_ref[:, _K0 + h * LANES:_K0 + (h + 1) * LANES], ck, sk)
            v = z_ref[:, _V0 + h * LANES:_V0 + (h + 1) * LANES].astype(BF16)
            qb, kb = q.astype(BF16), k.astype(BF16)
            qdb = (q * tile_rows(qd_ref[h])).astype(BF16)
            kdb = (k * tile_rows(kd_ref[h])).astype(BF16)
            local = [slice(c * RET_CHUNK, (c + 1) * RET_CHUNK) for c in range(nchunk)]
            scores = [_dot_nt(qb[r], kb[r]) for r in local]
            kvs = [_dot_tn(kdb[r], v[r]) for r in local]
            before = []
            for kv in kvs:
                before.append(states[h].astype(BF16))
                states[h] = states[h] * chunk_dec[h] + kv
            stage1.append((v, qdb, scores, before, local))
        for h, (v, qdb, scores, before, local) in enumerate(stage1):
            lanes = slice(h * LANES, (h + 1) * LANES)
            dec = dec_ref[h]
            for c, r in enumerate(local):
                lhs = jnp.concatenate([(scores[c] * dec).astype(BF16), qdb[r]], axis=1)
                rhs = jnp.concatenate([v[r], before[c]], axis=0)
                o = _dot(lhs, rhs)
                o = o * lax.rsqrt(jnp.mean(o * o, axis=-1, keepdims=True) + RMS_EPS)
                gate = _silu(z_ref[chunks[c], _G0 + h * LANES:_G0 + (h + 1) * LANES])
                cat_ref[chunks[c], lanes] = (gate * o).astype(BF16)

        for gi in range(GM_GROUPS):
            u, gvn = _gmlp_branch(z_ref, slice(None), lng_ref, lnb_ref, gi)
            gvb = gvn.astype(BF16)
            rhs = jnp.concatenate([gvb[c * GM_CHUNK:(c + 1) * GM_CHUNK] for c in range(nchunk)], axis=1)
            mixed = _dot(w_spatial[gi], rhs)
            bias = bs_ref[gi]
            for c in range(nchunk):
                local_rows = slice(c * GM_CHUNK, (c + 1) * GM_CHUNK)
                m = mixed[:, c * GM_CHUNK:(c + 1) * GM_CHUNK] + bias
                cat_ref[chunks[c], 512 + gi * LANES:512 + (gi + 1) * LANES] = (u[local_rows] * m).astype(BF16)

        o_ref[rows, :] = x + _dot(cat_ref[...], wo_ref[...])

    for h in range(RET_HEADS):
        s_ref[h] = states[h]

    @pl.when(si == pl.num_programs(1) - 1)
    def _():
        st_ref[...] = s_ref[...]


def _ab_prompt(x, g, w_in, w_o, w_s, b_s, ln_g, ln_b, batch, seq):
    nstep = seq // MIXER_ROWS
    decay, q_dec, k_dec, chunk_dec = _retention_tables(RET_CHUNK)
    rot = _rotary_tables(np.arange(seq))
    lane_b = lambda t: (np if isinstance(t, np.ndarray) else jnp).broadcast_to(t[:, :, None], t.shape + (LANES,))
    row_spec = pl.BlockSpec((MIXER_ROWS, D_MODEL), lambda b, s: (b * nstep + s, 0))
    rot_spec = pl.BlockSpec((MIXER_ROWS, 4 * LANES), lambda b, s: (s, 0))
    cube = (RET_HEADS, RET_CHUNK, LANES)
    out, state = pl.pallas_call(
        functools.partial(_ab_prompt_kernel, chunk_dec=chunk_dec),
        grid=(batch, nstep),
        in_specs=[
            row_spec,
            _const_spec((1, D_MODEL)),
            _const_spec((D_MODEL, AB_IN)),
            _const_spec((AB_OUT, D_MODEL)),
            rot_spec,
            _const_spec(cube), _const_spec(cube), _const_spec(cube),
            _const_spec(cube), _const_spec(cube),
            _const_spec((GM_GROUPS, GM_GROUP_DIM)), _const_spec((GM_GROUPS, GM_GROUP_DIM)),
        ],
        out_specs=[
            row_spec,
            pl.BlockSpec((None, RET_HEADS, RET_DK, RET_DV), lambda b, s: (b, 0, 0, 0)),
        ],
        out_shape=[
            jax.ShapeDtypeStruct((batch * seq, D_MODEL), F32),
            jax.ShapeDtypeStruct((batch, RET_HEADS, RET_DK, RET_DV), F32),
        ],
        scratch_shapes=(
            [pltpu.VMEM((MIXER_ROWS // MIXER_SPLIT, AB_IN), F32)] * MIXER_SPLIT
            + [pltpu.VMEM((MIXER_ROWS // MIXER_SPLIT, AB_OUT), BF16)] * MIXER_SPLIT
            + [pltpu.VMEM((RET_HEADS, RET_DK, RET_DV), F32)]),
        compiler_params=_params("arbitrary", "arbitrary"),
        name="ab_prompt",
    )(x, g, w_in, w_o, rot, decay, lane_b(q_dec), lane_b(k_dec),
      w_s, lane_b(b_s), ln_g, ln_b)
    return out, state


def _ab_sample_kernel(x_ref, g_ref, win_ref, wo_ref, rot_ref,
                      dec_ref, qd_ref, kd_ref, ws_ref, bs_ref, lng_ref, lnb_ref, causal_ref, st_ref,
                      o_ref, sto_ref, gvn_ref, z_ref, cat_ref, *, chunk_dec, dec_len):
    rows_n = SAMPLE_BB * dec_len
    x = x_ref[...]
    hn = _rms(x, g_ref[...]).astype(BF16)
    z_ref[...] = _dot(hn, win_ref[...])

    cq, sq, ck, sk = (rot_ref[:, i * LANES:(i + 1) * LANES] for i in range(4))
    token = lax.broadcasted_iota(jnp.int32, (rows_n, rows_n), 1)
    for h in range(RET_HEADS):
        lanes = slice(h * LANES, (h + 1) * LANES)
        q = _rotary(z_ref[:, _Q0 + h * LANES:_Q0 + (h + 1) * LANES], cq, sq)
        k = _rotary(z_ref[:, _K0 + h * LANES:_K0 + (h + 1) * LANES], ck, sk)
        v = z_ref[:, _V0 + h * LANES:_V0 + (h + 1) * LANES].astype(BF16)
        scores = _dot_nt(q.astype(BF16), k.astype(BF16)) * dec_ref[h]
        intra = _dot(scores.astype(BF16), v)
        qs = q * qd_ref[h]
        kst = (k * kd_ref[h]).T
        kst_all = jnp.concatenate(
            [jnp.where((token >= b * dec_len) & (token < (b + 1) * dec_len), kst, 0.0).astype(BF16)
             for b in range(SAMPLE_BB)], axis=0)
        kv_all = _dot(kst_all, v)
        cross = []
        for b in range(SAMPLE_BB):
            rows = slice(b * dec_len, (b + 1) * dec_len)
            state = st_ref[b, h]
            cross.append(_dot(qs[rows].astype(BF16), state.astype(BF16)))
            sto_ref[b, h] = state * chunk_dec[h] + kv_all[b * RET_DK:(b + 1) * RET_DK]
        o = intra + jnp.concatenate(cross, axis=0)
        o = o * lax.rsqrt(jnp.mean(o * o, axis=-1, keepdims=True) + RMS_EPS)
        gate = _silu(z_ref[:, _G0 + h * LANES:_G0 + (h + 1) * LANES])
        cat_ref[:, lanes] = (gate * o).astype(BF16)

    causal = causal_ref[...] > 0.0
    for gi in range(GM_GROUPS):
        u, gvn = _gmlp_branch(z_ref, slice(None), lng_ref, lnb_ref, gi)
        gvn_ref[:, gi * LANES:(gi + 1) * LANES] = gvn
        w = jnp.where(causal, ws_ref[gi], 0.0).astype(BF16)
        mixed = _dot(w, gvn.astype(BF16)) + bs_ref[gi]
        cat_ref[:, 512 + gi * LANES:512 + (gi + 1) * LANES] = (u * mixed).astype(BF16)

    o_ref[...] = x + _dot(cat_ref[...], wo_ref[...])


def _ab_sample(x, g, w_in, w_o, w_s, b_s, ln_g, ln_b, state, dec_batch, dec_len):
    rows_n = SAMPLE_BB * dec_len
    assert rows_n == LANES and dec_batch % SAMPLE_BB == 0
    decay, q_dec, k_dec, chunk_dec = _retention_tables(dec_len)
    rot = np.tile(_rotary_tables(PAST_LEN + np.arange(dec_len)), (SAMPLE_BB, 1))
    bd_decay = np.stack([np.kron(np.eye(SAMPLE_BB, dtype=np.float32), d) for d in decay])

    def rows_b(t):
        xp = np if isinstance(t, np.ndarray) else jnp
        return xp.broadcast_to(xp.tile(t, (1, SAMPLE_BB))[:, :, None], (t.shape[0], rows_n, LANES))
    w_tiled = jnp.tile(w_s[:, :dec_len, :dec_len], (1, SAMPLE_BB, SAMPLE_BB))
    causal = jnp.asarray(np.kron(np.eye(SAMPLE_BB), np.tril(np.ones((dec_len, dec_len)))), F32)
    row_spec = pl.BlockSpec((rows_n, D_MODEL), lambda i: (i, 0))
    st_spec = pl.BlockSpec((SAMPLE_BB, RET_HEADS, RET_DK, RET_DV), lambda i: (i, 0, 0, 0))
    cube = (RET_HEADS, rows_n, LANES)
    tab = (rows_n, LANES)
    return pl.pallas_call(
        functools.partial(_ab_sample_kernel, chunk_dec=chunk_dec, dec_len=dec_len),
        grid=(dec_batch // SAMPLE_BB,),
        in_specs=[
            row_spec,
            _const_spec((1, D_MODEL)),
            _const_spec((D_MODEL, AB_IN)),
            _const_spec((AB_OUT, D_MODEL)),
            _const_spec((rows_n, 4 * LANES)),
            _const_spec(cube), _const_spec(cube), _const_spec(cube),
            _const_spec(cube), _const_spec(cube),
            _const_spec((GM_GROUPS, GM_GROUP_DIM)), _const_spec((GM_GROUPS, GM_GROUP_DIM)),
            _const_spec(tab),
            st_spec,
        ],
        out_specs=[
            row_spec,
            st_spec,
            pl.BlockSpec((rows_n, GM_GROUPS * GM_GROUP_DIM), lambda i: (i, 0)),
        ],
        out_shape=[
            jax.ShapeDtypeStruct((dec_batch * dec_len, D_MODEL), F32),
            jax.ShapeDtypeStruct(state.shape, F32),
            jax.ShapeDtypeStruct((dec_batch * dec_len, GM_GROUPS * GM_GROUP_DIM), F32),
        ],
        scratch_shapes=[
            pltpu.VMEM((rows_n, AB_IN), F32),
            pltpu.VMEM((rows_n, AB_OUT), BF16),
        ],
        compiler_params=_params("arbitrary"),
        name="ab_sample",
    )(x, g, w_in, w_o, rot, bd_decay, rows_b(q_dec), rows_b(k_dec),
      w_tiled, rows_b(b_s[:, :dec_len]), ln_g, ln_b, causal, state)


_LOG2E = float(np.log2(np.e))


def _softmax_numerators(s, sink_even, sink_odd):
    s0, s1, s2, s3 = (s[:, i * LANES:(i + 1) * LANES] for i in range(4))
    m_even = jnp.max(jnp.maximum(jnp.maximum(s0, s1), sink_even), axis=-1, keepdims=True)
    m_odd = jnp.max(jnp.maximum(jnp.maximum(s2, s3), sink_odd), axis=-1, keepdims=True)
    p = jnp.concatenate([jnp.exp2(s0 - m_even), jnp.exp2(s1 - m_even),
                         jnp.exp2(s2 - m_odd), jnp.exp2(s3 - m_odd)], axis=1)
    even_lane = lax.broadcasted_iota(jnp.int32, sink_even.shape, 1) < SWA_HD
    sink_term = jnp.where(even_lane, jnp.exp2(sink_even - m_even), jnp.exp2(sink_odd - m_odd))
    return p.astype(BF16), sink_term


def _band_bias(q_rows, first_key_row):
    i = np.arange(q_rows)[:, None]
    j = np.arange(2 * WINDOW)[None, :]
    rel = i + first_key_row - j
    half = np.where((rel >= 0) & (rel <= WINDOW), 0.0, -np.inf).astype(np.float32)
    return np.concatenate([half, half], axis=1)


def _kv_layouts(kcol, vcol):
    even_lane = lax.broadcasted_iota(jnp.int32, kcol.shape, 1) < SWA_HD
    ks = kcol * (SWA_HD ** -0.5 * _LOG2E)
    kr = pltpu.roll(ks, SWA_HD, 1)
    vr = pltpu.roll(vcol, SWA_HD, 1)
    one_even = jnp.where(even_lane, 1.0, 0.0)
    one_odd = 1.0 - one_even
    zero = jnp.zeros_like(kcol)
    out = []
    for kvh in range(SWA_KV_HEADS):
        k_lo, k_hi = (ks, kr) if kvh == 0 else (kr, ks)
        v_lo, v_hi = (vcol, vr) if kvh == 0 else (vr, vcol)
        ktop = jnp.where(even_lane, k_lo, zero).astype(BF16)
        kbot = jnp.where(even_lane, zero, k_hi).astype(BF16)
        vtop = jnp.concatenate([jnp.where(even_lane, v_lo, zero), one_even], axis=1).astype(BF16)
        vbot = jnp.concatenate([jnp.where(even_lane, zero, v_hi), one_odd], axis=1).astype(BF16)
        out.append((ktop, kbot, vtop, vbot))
    return out


def _swa_prompt_kernel(x_ref, xh_ref, g_ref, wqkv_ref, bqkv_ref, wo_ref, bo_ref, sink_ref, bias_ref,
                       o_ref, kc_ref, vc_ref, z_ref, cat_ref):
    si = pl.program_id(1)
    nblk = MIXER_ROWS // WINDOW
    npair = SWA_HEADS // SWA_KV_HEADS // 2
    sub_rows = MIXER_ROWS // MIXER_SPLIT

    x = x_ref[...]
    hn = _rms(x, g_ref[...]).astype(BF16)
    for sub in range(MIXER_SPLIT):
        rows = slice(sub * sub_rows, (sub + 1) * sub_rows)
        z_ref[rows, :] = _dot(hn[rows], wqkv_ref[...]) + bqkv_ref[...]
    hh = _rms(xh_ref[...], g_ref[...]).astype(BF16)
    zh = _dot(hh, wqkv_ref[:, _SK0:]) + bqkv_ref[:, _SK0:]

    kcol = z_ref[:, _SK0:_SK0 + LANES]
    vcol = z_ref[:, _SV0:_SV0 + LANES]
    kc_ref[...] = kcol[MIXER_ROWS - WINDOW:]
    vc_ref[...] = vcol[MIXER_ROWS - WINDOW:]
    layouts = _kv_layouts(jnp.concatenate([zh[:, :LANES], kcol], axis=0),
                          jnp.concatenate([zh[:, LANES:], vcol], axis=0))

    for r in range(nblk):
        rows = slice(r * WINDOW, (r + 1) * WINDOW)
        keys = slice(r * WINDOW, (r + 2) * WINDOW)
        tile = bias_ref[jnp.where(si > 0, 0, 1)] if r == 0 else bias_ref[0]
        bias = jnp.concatenate([tile] * npair, axis=0)
        for kvh, (ktop, kbot, vtop, vbot) in enumerate(layouts):
            c0 = kvh * npair * LANES
            lhs = jnp.concatenate(
                [z_ref[rows, c0 + p * LANES:c0 + (p + 1) * LANES] for p in range(npair)], axis=0).astype(BF16)
            s = _dot_nt(lhs, jnp.concatenate([ktop[keys], kbot[keys]], axis=0)) + bias
            p, sink_term = _softmax_numerators(s, sink_ref[kvh, 0], sink_ref[kvh, 1])
            out = _dot(p, jnp.concatenate([vtop[keys], vbot[keys]], axis=0))
            o = out[:, :LANES] / (out[:, LANES:] + sink_term)
            for p_i in range(npair):
                cat_ref[rows, c0 + p_i * LANES:c0 + (p_i + 1) * LANES] = (
                    o[p_i * WINDOW:(p_i + 1) * WINDOW].astype(BF16))

    o_ref[...] = x + _dot(cat_ref[...], wo_ref[...]) + bo_ref[...]


def _sink_table(sinks, rows_per_pair):
    npair = SWA_HEADS // SWA_KV_HEADS // 2
    s = (sinks.astype(F32) * _LOG2E).reshape(SWA_KV_HEADS, npair, 2).transpose(0, 2, 1)[:, :, :, None, None]
    s = jnp.broadcast_to(s, (SWA_KV_HEADS, 2, npair, rows_per_pair, LANES))
    return s.reshape(SWA_KV_HEADS, 2, npair * rows_per_pair, LANES)


def _swa_prompt(x, g, w_qkv, b_qkv, sinks, w_o, b_o, batch, seq):
    nstep = seq // MIXER_ROWS
    npair = SWA_HEADS // SWA_KV_HEADS // 2
    row_spec = pl.BlockSpec((MIXER_ROWS, D_MODEL), lambda b, s: (b * nstep + s, 0))
    nblk = MIXER_ROWS // WINDOW
    halo_spec = pl.BlockSpec((WINDOW, D_MODEL), lambda b, s: (jnp.maximum((b * nstep + s) * nblk - 1, 0), 0))
    cache_spec = pl.BlockSpec((None, WINDOW, LANES), lambda b, s: (b, 0, 0))
    band = _band_bias(WINDOW, WINDOW)
    no_prev = np.where(np.arange(4 * WINDOW)[None, :] % (2 * WINDOW) >= WINDOW, band, -np.inf).astype(np.float32)
    bias = jnp.asarray(np.stack([band, no_prev]))
    return pl.pallas_call(
        _swa_prompt_kernel,
        grid=(batch, nstep),
        in_specs=[
            row_spec,
            halo_spec,
            _const_spec((1, D_MODEL)),
            _const_spec((D_MODEL, SWA_IN)),
            _const_spec((1, SWA_IN)),
            _const_spec((SWA_OUT, D_MODEL)),
            _const_spec((1, D_MODEL)),
            _const_spec((SWA_KV_HEADS, 2, npair * WINDOW, LANES)),
            _const_spec((2, WINDOW, 4 * WINDOW)),
        ],
        out_specs=[row_spec, cache_spec, cache_spec],
        out_shape=[
            jax.ShapeDtypeStruct((batch * seq, D_MODEL), F32),
            jax.ShapeDtypeStruct((batch, WINDOW, LANES), F32),
            jax.ShapeDtypeStruct((batch, WINDOW, LANES), F32),
        ],
        scratch_shapes=[
            pltpu.VMEM((MIXER_ROWS, SWA_IN), F32),
            pltpu.VMEM((MIXER_ROWS, SWA_OUT), BF16),
        ],
        compiler_params=_params("arbitrary", "arbitrary"),
        name="swa_prompt",
    )(x, x, g, w_qkv, b_qkv, w_o, b_o, _sink_table(sinks, WINDOW), bias)


def _swa_sample_kernel(x_ref, g_ref, wqkv_ref, bqkv_ref, wo_ref, bo_ref, sink_ref, bias_ref, ck_ref, cv_ref,
                       o_ref, nk_ref, nv_ref, z_ref, cat_ref, *, dec_len):
    ncol = SWA_HEADS // 2
    grp = ncol * dec_len
    x = x_ref[...]
    hn = _rms(x, g_ref[...]).astype(BF16)
    z_ref[...] = _dot(hn, wqkv_ref[...]) + bqkv_ref[...]

    bias, sink = bias_ref[...], sink_ref[...]
    lane_lo = lax.broadcasted_iota(jnp.int32, (grp, LANES), 1) < SWA_HD
    kv0 = lax.broadcasted_iota(jnp.int32, (grp, LANES), 0) < grp // 2
    own_half = jnp.where(kv0, 0, 1) == jnp.where(lane_lo, 0, 1)
    pad = jnp.zeros((WINDOW - dec_len, LANES), F32)
    elems = [slice(b * dec_len, (b + 1) * dec_len) for b in range(SAMPLE_BB)]
    values, scores = [], []
    for b, rows in enumerate(elems):
        knew = z_ref[rows, _SK0:_SK0 + LANES]
        vnew = z_ref[rows, _SV0:_SV0 + LANES]
        kold, vold = ck_ref[b], cv_ref[b]
        nk_ref[b, 0:WINDOW - dec_len, :] = kold[dec_len:]
        nk_ref[b, WINDOW - dec_len:, :] = knew
        nv_ref[b, 0:WINDOW - dec_len, :] = vold[dec_len:]
        nv_ref[b, WINDOW - dec_len:, :] = vnew
        kall = jnp.concatenate([kold, knew, pad], axis=0).astype(BF16)
        values.append(jnp.concatenate([vold, vnew, pad], axis=0).astype(BF16))

        q = jnp.concatenate([z_ref[rows, j * LANES:(j + 1) * LANES] for j in range(ncol)], axis=0)
        q = q * (SWA_HD ** -0.5 * _LOG2E)
        q_swapped = pltpu.roll(q, SWA_HD, 1)
        q_even = jnp.where(own_half, jnp.where(kv0, q, q_swapped), 0.0)
        q_odd = jnp.where(own_half, jnp.where(kv0, q_swapped, q), 0.0)
        lhs = jnp.concatenate([q_even, q_odd], axis=0).astype(BF16)
        scores.append(_dot_nt(lhs, kall) + bias)

    probs, dens = [], []
    for s in scores:
        m = jnp.max(jnp.maximum(jnp.maximum(s[:, :LANES], s[:, LANES:]), sink), axis=-1, keepdims=True)
        p = jnp.exp2(s - m)
        dens.append(jnp.sum(p, axis=-1, keepdims=True) + jnp.exp2(sink - m))
        probs.append(p.astype(BF16))

    outs = [_dot(p, v) for p, v in zip(probs, values)]

    for rows, out, den in zip(elems, outs, dens):
        o = out / den
        o_swapped = pltpu.roll(o, SWA_HD, 1)
        cols = jnp.where(lane_lo,
                         jnp.where(kv0, o[:grp], o_swapped[:grp]),
                         jnp.where(kv0, o_swapped[grp:], o[grp:]))
        for j in range(ncol):
            cat_ref[rows, j * LANES:(j + 1) * LANES] = cols[j * dec_len:(j + 1) * dec_len]

    o_ref[...] = x + _dot(cat_ref[...].astype(BF16), wo_ref[...]) + bo_ref[...]


def _swa_sample(x, g, w_qkv, b_qkv, sinks, w_o, b_o, cache_k, cache_v, dec_batch, dec_len):
    rows_n = SAMPLE_BB * dec_len
    ncol = SWA_HEADS // 2
    score_rows = 2 * ncol * dec_len
    sink = jnp.broadcast_to((sinks.astype(F32) * _LOG2E).reshape(ncol, 2).T[:, :, None, None],
                            (2, ncol, dec_len, LANES)).reshape(score_rows, LANES)
    bias = jnp.asarray(np.tile(_band_bias(dec_len, WINDOW)[:, :2 * WINDOW], (2 * ncol, 1)))
    row_spec = pl.BlockSpec((rows_n, D_MODEL), lambda i: (i, 0))
    cache_spec = pl.BlockSpec((SAMPLE_BB, WINDOW, LANES), lambda i: (i, 0, 0))
    return pl.pallas_call(
        functools.partial(_swa_sample_kernel, dec_len=dec_len),
        grid=(dec_batch // SAMPLE_BB,),
        in_specs=[
            row_spec,
            _const_spec((1, D_MODEL)),
            _const_spec((D_MODEL, SWA_IN)),
            _const_spec((1, SWA_IN)),
            _const_spec((SWA_OUT, D_MODEL)),
            _const_spec((1, D_MODEL)),
            _const_spec((score_rows, LANES)),
            _const_spec((score_rows, 2 * WINDOW)),
            cache_spec, cache_spec,
        ],
        out_specs=[row_spec, cache_spec, cache_spec],
        out_shape=[
            jax.ShapeDtypeStruct((dec_batch * dec_len, D_MODEL), F32),
            jax.ShapeDtypeStruct((dec_batch, WINDOW, LANES), F32),
            jax.ShapeDtypeStruct((dec_batch, WINDOW, LANES), F32),
        ],
        scratch_shapes=[
            pltpu.VMEM((rows_n, SWA_IN), F32),
            pltpu.VMEM((rows_n, SWA_OUT), F32),
        ],
        compiler_params=_params("arbitrary"),
        name="swa_sample",
    )(x, g, w_qkv, b_qkv, w_o, b_o, sink, bias, cache_k, cache_v)


def kernel(x_prompt, x_sample, state_ret, cache_swa_k, cache_swa_v, norm_mix, norm_mlp, norm_final, ab_w_in, ab_w_s, ab_b_s, ab_ln_g, ab_ln_b, ab_w_o, swa_w_qkv, swa_b_qkv, swa_sinks, swa_w_o, swa_b_o, mlp_w_up, mlp_w_down):
    batch, seq, _ = x_prompt.shape
    dec_batch, dec_len, _ = x_sample.shape
    assert seq % MIXER_ROWS == 0 and seq % RET_CHUNK == 0
    kv_lanes = SWA_KV_HEADS * SWA_HD

    hp = x_prompt.reshape(batch * seq, D_MODEL)
    hs = x_sample.reshape(dec_batch * dec_len, D_MODEL)
    row = lambda v: v.reshape(1, -1)
    gfin = row(norm_final)

    w_in, w_o = ab_w_in[0].astype(BF16), ab_w_o[0].astype(BF16)
    hp, ret_p = _ab_prompt(hp, row(norm_mix[0]), w_in, w_o, ab_w_s[0], ab_b_s[0], ab_ln_g[0], ab_ln_b[0],
                           batch, seq)
    hs, ret_s, gm_s = _ab_sample(hs, row(norm_mix[0]), w_in, w_o, ab_w_s[0], ab_b_s[0], ab_ln_g[0], ab_ln_b[0],
                                 state_ret[0], dec_batch, dec_len)
    w_up, w_dn = mlp_w_up[0].astype(BF16), mlp_w_down[0].astype(BF16)
    hp = _mlp(hp, row(norm_mlp[0]), w_up, w_dn, gfin, final_norm=False, name="mlp0_prompt")
    hs = _mlp(hs, row(norm_mlp[0]), w_up, w_dn, gfin, final_norm=False, name="mlp0_sample")

    w_qkv, w_so = swa_w_qkv[0].astype(BF16), swa_w_o[0].astype(BF16)
    hp, kp, vp = _swa_prompt(hp, row(norm_mix[1]), w_qkv, row(swa_b_qkv[0]), swa_sinks[0], w_so, row(swa_b_o[0]),
                             batch, seq)
    hs, ks, vs = _swa_sample(hs, row(norm_mix[1]), w_qkv, row(swa_b_qkv[0]), swa_sinks[0], w_so, row(swa_b_o[0]),
                             cache_swa_k[0].reshape(dec_batch, WINDOW, kv_lanes),
                             cache_swa_v[0].reshape(dec_batch, WINDOW, kv_lanes), dec_batch, dec_len)
    w_up, w_dn = mlp_w_up[1].astype(BF16), mlp_w_down[1].astype(BF16)
    hp = _mlp(hp, row(norm_mlp[1]), w_up, w_dn, gfin, final_norm=True, name="mlp1_prompt")
    hs = _mlp(hs, row(norm_mlp[1]), w_up, w_dn, gfin, final_norm=True, name="mlp1_sample")

    cache_shape = lambda n: (1, n, WINDOW, SWA_KV_HEADS, SWA_HD)
    return (hp.reshape(batch, seq, D_MODEL),
            hs.reshape(dec_batch, dec_len, D_MODEL),
            ret_p[None],
            ret_s[None],
            gm_s.reshape(1, dec_batch, dec_len, GM_GROUPS * GM_GROUP_DIM),
            kp.reshape(cache_shape(batch)),
            vp.reshape(cache_shape(batch)),
            ks.reshape(cache_shape(dec_batch)),
            vs.reshape(cache_shape(dec_batch)))
```

```python
import functools

import numpy as np
import jax
import jax.numpy as jnp
from jax import lax
from jax.experimental import pallas as pl
from jax.experimental.pallas import tpu as pltpu

F32 = jnp.float32
BF16 = jnp.bfloat16

D_MODEL = 1024
PAST_LEN = 16384
RMS_EPS = 1e-6
LN_EPS = 1e-5
RET_HEADS = 4
RET_DK = 128
RET_DV = 128
RET_CHUNK = 128
ROPE_BASE = 10000.0
GM_GROUPS = 4
GM_GROUP_DIM = 128
GM_CHUNK = 128
AB_IN = 3072
AB_OUT = 1024
SWA_HEADS = 16
SWA_KV_HEADS = 2
SWA_HD = 64
WINDOW = 128
SWA_IN = 1280
SWA_OUT = 1024
D_FF = 4096

_Q0, _K0, _V0, _G0, _U0, _GV0 = 0, 512, 1024, 1536, 2048, 2560
_SK0, _SV0 = 1024, 1152

LANES = 128
MIXER_ROWS = 1024
MIXER_SPLIT = 4
MLP_ROWS = 1024
MLP_SPLIT = 4
FF_CHUNK = 1024
SAMPLE_BB = 16
VMEM_LIMIT = 56 * 1024 * 1024


def _rms(x, g):
    return x * lax.rsqrt(jnp.mean(x * x, axis=-1, keepdims=True) + RMS_EPS) * g


def _gelu(x):
    return 0.5 * x * (1.0 + lax.erf(x * np.float32(np.sqrt(0.5))))


def _silu(x):
    return x / (1.0 + jnp.exp(-x))


def _dot(a, b):
    return jnp.dot(a, b, preferred_element_type=F32)


def _dot_nt(a, b):
    return lax.dot_general(a, b, (((1,), (1,)), ((), ())), preferred_element_type=F32)


def _dot_tn(a, b):
    return lax.dot_general(a, b, (((0,), (0,)), ((), ())), preferred_element_type=F32)


def _const_spec(shape):
    zeros = (0,) * len(shape)
    return pl.BlockSpec(shape, lambda *_: zeros, pipeline_mode=pl.Buffered(1))


def _params(*sem):
    return pltpu.CompilerParams(dimension_semantics=sem, vmem_limit_bytes=VMEM_LIMIT)


def _cast_specs(weights, steps, flat_step):
    def src_map(layer):
        return lambda *g: (layer, flat_step(*g), 0)

    in_specs, out_specs, out_shapes = [], [], []
    for w, layer in weights:
        _, rows, cols = w.shape
        slab = rows // steps
        assert slab * steps == rows and slab % 16 == 0, (w.shape, steps)
        in_specs.append(pl.BlockSpec((None, slab, cols), src_map(layer)))
        out_specs.append(pl.BlockSpec((slab, cols), lambda *g: (flat_step(*g), 0)))
        out_shapes.append(jax.ShapeDtypeStruct((rows, cols), BF16))
    return in_specs, out_specs, out_shapes


def _cast_slabs(src_refs, dst_refs):
    for src, dst in zip(src_refs, dst_refs):
        dst[...] = src[...].astype(BF16)


def _mlp_kernel(x_ref, g_ref, wup_ref, wdn_ref, gf_ref, *rest, final_norm, n_cast):
    cast_src, o_ref, cast_dst = rest[:n_cast], rest[n_cast], rest[n_cast + 1:]
    _cast_slabs(cast_src, cast_dst)
    rows = x_ref.shape[0] // MLP_SPLIT
    groups = [slice(i * rows, (i + 1) * rows) for i in range(MLP_SPLIT)]
    xs = [x_ref[r, :] for r in groups]
    hns = [_rms(x, g_ref[...]).astype(BF16) for x in xs]
    accs = [jnp.zeros_like(x) for x in xs]
    for c in range(D_FF // FF_CHUNK):
        cols = slice(c * FF_CHUNK, (c + 1) * FF_CHUNK)
        for i in range(MLP_SPLIT):
            h = _dot(hns[i], wup_ref[:, cols])
            h = jnp.square(jnp.maximum(h, 0.0)).astype(BF16)
            accs[i] = accs[i] + _dot(h, wdn_ref[cols, :])
    for i, r in enumerate(groups):
        y = xs[i] + accs[i]
        if final_norm:
            y = _rms(y, gf_ref[...])
        o_ref[r, :] = y


def _mlp(x, g, wup, wdn, gf, *, final_norm, name, cast=()):
    rows = x.shape[0]
    tm = min(MLP_ROWS, rows)
    steps = rows // tm
    row_spec = pl.BlockSpec((tm, D_MODEL), lambda i: (i, 0))
    cast_in, cast_out, cast_shapes = _cast_specs(cast, steps, lambda i: i)
    out, *cast_w = pl.pallas_call(
        functools.partial(_mlp_kernel, final_norm=final_norm, n_cast=len(cast)),
        grid=(steps,),
        in_specs=[
            row_spec,
            _const_spec((1, D_MODEL)),
            _const_spec((D_MODEL, D_FF)),
            _const_spec((D_FF, D_MODEL)),
            _const_spec((1, D_MODEL)),
            *cast_in,
        ],
        out_specs=[row_spec, *cast_out],
        out_shape=[jax.ShapeDtypeStruct((rows, D_MODEL), F32), *cast_shapes],
        compiler_params=_params("arbitrary"),
        name=name,
    )(x, g, wup, wdn, gf, *(w for w, _ in cast))
    return out, cast_w


def _rotary(x, cos2, sin2):
    return x * cos2 + pltpu.roll(x, 64, 1) * sin2


def _retention_tables(chunk):
    log_g = np.log1p(-np.exp2(-5.0 - np.arange(RET_HEADS, dtype=np.float64)))
    idx = np.arange(chunk, dtype=np.float64)
    diff = idx[:, None] - idx[None, :]
    decay = np.where(diff[None] >= 0, np.exp(log_g[:, None, None] * np.maximum(diff, 0.0)[None]), 0.0)
    q_dec = np.exp(log_g[:, None] * (idx[None, :] + 1.0))
    k_dec = np.exp(log_g[:, None] * (chunk - 1.0 - idx[None, :]))
    chunk_dec = tuple(float(v) for v in np.exp(log_g * chunk))
    f32 = lambda t: t.astype(np.float32)
    return f32(decay), f32(q_dec), f32(k_dec), chunk_dec


def _rotary_tables(pos):
    half = RET_DK // 2
    inv = ROPE_BASE ** (-np.arange(half, dtype=np.float64) / half)
    ang = np.asarray(pos, np.float64)[:, None] * inv[None, :]
    cos, sin = np.cos(ang), np.sin(ang)
    cos2 = np.concatenate([cos, cos], axis=-1)
    sin2 = np.concatenate([-sin, sin], axis=-1)
    scale = RET_DK ** -0.5
    return np.concatenate([cos2 * scale, sin2 * scale, cos2, sin2], axis=-1).astype(np.float32)


def _gmlp_branch(z_ref, rows, lng_ref, lnb_ref, gi):
    u = _gelu(z_ref[rows, _U0 + gi * LANES:_U0 + (gi + 1) * LANES])
    gv = _gelu(z_ref[rows, _GV0 + gi * LANES:_GV0 + (gi + 1) * LANES])
    mu = jnp.mean(gv, axis=-1, keepdims=True)
    cen = gv - mu
    var = jnp.mean(cen * cen, axis=-1, keepdims=True)
    gvn = cen * lax.rsqrt(var + LN_EPS) * lng_ref[gi:gi + 1, :] + lnb_ref[gi:gi + 1, :]
    return u, gvn


def _ab_prompt_kernel(x_ref, g_ref, win_ref, wo_ref, rot_ref,
                      dec_ref, qd_ref, kd_ref, ws_ref, bs_ref, lng_ref, lnb_ref,
                      *rest, chunk_dec, n_cast):
    cast_src, (o_ref, st_ref), cast_dst = rest[:n_cast], rest[n_cast:n_cast + 2], rest[n_cast + 2:2 * n_cast + 2]
    scratch = rest[2 * n_cast + 2:]
    _cast_slabs(cast_src, cast_dst)
    si = pl.program_id(1)
    z_refs, cat_refs, s_ref = scratch[:MIXER_SPLIT], scratch[MIXER_SPLIT:2 * MIXER_SPLIT], scratch[-1]
    sub_rows = MIXER_ROWS // MIXER_SPLIT
    nchunk = sub_rows // RET_CHUNK

    @pl.when(si == 0)
    def _():
        s_ref[...] = jnp.zeros_like(s_ref)

    ri = lax.broadcasted_iota(jnp.int32, (GM_CHUNK, GM_CHUNK), 0)
    ci = lax.broadcasted_iota(jnp.int32, (GM_CHUNK, GM_CHUNK), 1)
    w_spatial = [jnp.where(ci <= ri, ws_ref[gi], 0.0).astype(BF16) for gi in range(GM_GROUPS)]
    tile_rows = lambda t: jnp.concatenate([t] * nchunk, axis=0)
    states = [s_ref[h] for h in range(RET_HEADS)]

    for sub in range(MIXER_SPLIT):
        rows = slice(sub * sub_rows, (sub + 1) * sub_rows)
        chunks = [slice(c * RET_CHUNK, (c + 1) * RET_CHUNK) for c in range(nchunk)]
        z_ref, cat_ref = z_refs[sub], cat_refs[sub]
        x = x_ref[rows, :]
        hn = _rms(x, g_ref[...]).astype(BF16)
        z_ref[:, :_U0] = _dot(hn, win_ref[:, :_U0])
        z_ref[:, _U0:] = _dot(hn, win_ref[:, _U0:])

        cq, sq, ck, sk = (rot_ref[rows, i * LANES:(i + 1) * LANES] for i in range(4))
        stage1 = []
        for h in range(RET_HEADS):
            q = _rotary(z_ref[:, _Q0 + h * LANES:_Q0 + (h + 1) * LANES], cq, sq)
            k = _rotary(z_ref[:, _K0 + h * LANES:_K0 + (h + 1) * LANES], ck, sk)
            v = z_ref[:, _V0 + h * LANES:_V0 + (h + 1) * LANES].astype(BF16)
            qb, kb = q.astype(BF16), k.astype(BF16)
            qdb = (q * tile_rows(qd_ref[h])).astype(BF16)
            kdb = (k * tile_rows(kd_ref[h])).astype(BF16)
            local = [slice(c * RET_CHUNK, (c + 1) * RET_CHUNK) for c in range(nchunk)]
            scores = [_dot_nt(qb[r], kb[r]) for r in local]
            kvs = [_dot_tn(kdb[r], v[r]) for r in local]
            before = []
            for kv in kvs:
                before.append(states[h].astype(BF16))
                states[h] = states[h] * chunk_dec[h] + kv
            stage1.append((v, qdb, scores, before, local))
        for h, (v, qdb, scores, before, local) in enumerate(stage1):
            lanes = slice(h * LANES, (h + 1) * LANES)
            dec = dec_ref[h]
            for c, r in enumerate(local):
                lhs = jnp.concatenate([(scores[c] * dec).astype(BF16), qdb[r]], axis=1)
                rhs = jnp.concatenate([v[r], before[c]], axis=0)
                o = _dot(lhs, rhs)
                o = o * lax.rsqrt(jnp.mean(o * o, axis=-1, keepdims=True) + RMS_EPS)
                gate = _silu(z_ref[chunks[c], _G0 + h * LANES:_G0 + (h + 1) * LANES])
                cat_ref[chunks[c], lanes] = (gate * o).astype(BF16)

        for gi in range(GM_GROUPS):
            u, gvn = _gmlp_branch(z_ref, slice(None), lng_ref, lnb_ref, gi)
            gvb = gvn.astype(BF16)
            rhs = jnp.concatenate([gvb[c * GM_CHUNK:(c + 1) * GM_CHUNK] for c in range(nchunk)], axis=1)
            mixed = _dot(w_spatial[gi], rhs)
            bias = bs_ref[gi]
            for c in range(nchunk):
                local_rows = slice(c * GM_CHUNK, (c + 1) * GM_CHUNK)
                m = mixed[:, c * GM_CHUNK:(c + 1) * GM_CHUNK] + bias
                cat_ref[chunks[c], 512 + gi * LANES:512 + (gi + 1) * LANES] = (u[local_rows] * m).astype(BF16)

        o_ref[rows, :] = x + _dot(cat_ref[...], wo_ref[...])

    for h in range(RET_HEADS):
        s_ref[h] = states[h]

    @pl.when(si == pl.num_programs(1) - 1)
    def _():
        st_ref[...] = s_ref[...]


def _ab_prompt(x, g, w_in, w_o, w_s, b_s, ln_g, ln_b, batch, seq, cast=()):
    nstep = seq // MIXER_ROWS
    cast_in, cast_out, cast_shapes = _cast_specs(cast, batch * nstep, lambda b, s: b * nstep + s)
    decay, q_dec, k_dec, chunk_dec = _retention_tables(RET_CHUNK)
    rot = _rotary_tables(np.arange(seq))
    lane_b = lambda t: (np if isinstance(t, np.ndarray) else jnp).broadcast_to(t[:, :, None], t.shape + (LANES,))
    row_spec = pl.BlockSpec((MIXER_ROWS, D_MODEL), lambda b, s: (b * nstep + s, 0))
    rot_spec = pl.BlockSpec((MIXER_ROWS, 4 * LANES), lambda b, s: (s, 0))
    cube = (RET_HEADS, RET_CHUNK, LANES)
    out, state, *cast_w = pl.pallas_call(
        functools.partial(_ab_prompt_kernel, chunk_dec=chunk_dec, n_cast=len(cast)),
        grid=(batch, nstep),
        in_specs=[
            row_spec,
            _const_spec((1, D_MODEL)),
            _const_spec((D_MODEL, AB_IN)),
            _const_spec((AB_OUT, D_MODEL)),
            rot_spec,
            _const_spec(cube), _const_spec(cube), _const_spec(cube),
            _const_spec(cube), _const_spec(cube),
            _const_spec((GM_GROUPS, GM_GROUP_DIM)), _const_spec((GM_GROUPS, GM_GROUP_DIM)),
            *cast_in,
        ],
        out_specs=[
            row_spec,
            pl.BlockSpec((None, RET_HEADS, RET_DK, RET_DV), lambda b, s: (b, 0, 0, 0)),
            *cast_out,
        ],
        out_shape=[
            jax.ShapeDtypeStruct((batch * seq, D_MODEL), F32),
            jax.ShapeDtypeStruct((batch, RET_HEADS, RET_DK, RET_DV), F32),
            *cast_shapes,
        ],
        scratch_shapes=(
            [pltpu.VMEM((MIXER_ROWS // MIXER_SPLIT, AB_IN), F32)] * MIXER_SPLIT
            + [pltpu.VMEM((MIXER_ROWS // MIXER_SPLIT, AB_OUT), BF16)] * MIXER_SPLIT
            + [pltpu.VMEM((RET_HEADS, RET_DK, RET_DV), F32)]),
        compiler_params=_params("arbitrary", "arbitrary"),
        name="ab_prompt",
    )(x, g, w_in, w_o, rot, decay, lane_b(q_dec), lane_b(k_dec),
      w_s, lane_b(b_s), ln_g, ln_b, *(w for w, _ in cast))
    return out, state, cast_w


def _ab_sample_kernel(x_ref, g_ref, win_ref, wo_ref, rot_ref,
                      dec_ref, qd_ref, kd_ref, ws_ref, bs_ref, lng_ref, lnb_ref, causal_ref, st_ref,
                      o_ref, sto_ref, gvn_ref, z_ref, cat_ref, *, chunk_dec, dec_len):
    rows_n = SAMPLE_BB * dec_len
    x = x_ref[...]
    hn = _rms(x, g_ref[...]).astype(BF16)
    z_ref[...] = _dot(hn, win_ref[...])

    cq, sq, ck, sk = (rot_ref[:, i * LANES:(i + 1) * LANES] for i in range(4))
    token = lax.broadcasted_iota(jnp.int32, (rows_n, rows_n), 1)
    for h in range(RET_HEADS):
        lanes = slice(h * LANES, (h + 1) * LANES)
        q = _rotary(z_ref[:, _Q0 + h * LANES:_Q0 + (h + 1) * LANES], cq, sq)
        k = _rotary(z_ref[:, _K0 + h * LANES:_K0 + (h + 1) * LANES], ck, sk)
        v = z_ref[:, _V0 + h * LANES:_V0 + (h + 1) * LANES].astype(BF16)
        scores = _dot_nt(q.astype(BF16), k.astype(BF16)) * dec_ref[h]
        intra = _dot(scores.astype(BF16), v)
        qs = q * qd_ref[h]
        kst = (k * kd_ref[h]).T
        kst_all = jnp.concatenate(
            [jnp.where((token >= b * dec_len) & (token < (b + 1) * dec_len), kst, 0.0).astype(BF16)
             for b in range(SAMPLE_BB)], axis=0)
        kv_all = _dot(kst_all, v)
        cross = []
        for b in range(SAMPLE_BB):
            rows = slice(b * dec_len, (b + 1) * dec_len)
            state = st_ref[b, h]
            cross.append(_dot(qs[rows].astype(BF16), state.astype(BF16)))
            sto_ref[b, h] = state * chunk_dec[h] + kv_all[b * RET_DK:(b + 1) * RET_DK]
        o = intra + jnp.concatenate(cross, axis=0)
        o = o * lax.rsqrt(jnp.mean(o * o, axis=-1, keepdims=True) + RMS_EPS)
        gate = _silu(z_ref[:, _G0 + h * LANES:_G0 + (h + 1) * LANES])
        cat_ref[:, lanes] = (gate * o).astype(BF16)

    causal = causal_ref[...] > 0.0
    for gi in range(GM_GROUPS):
        u, gvn = _gmlp_branch(z_ref, slice(None), lng_ref, lnb_ref, gi)
        gvn_ref[:, gi * LANES:(gi + 1) * LANES] = gvn
        w = jnp.where(causal, ws_ref[gi], 0.0).astype(BF16)
        mixed = _dot(w, gvn.astype(BF16)) + bs_ref[gi]
        cat_ref[:, 512 + gi * LANES:512 + (gi + 1) * LANES] = (u * mixed).astype(BF16)

    o_ref[...] = x + _dot(cat_ref[...], wo_ref[...])


def _ab_sample(x, g, w_in, w_o, w_s, b_s, ln_g, ln_b, state, dec_batch, dec_len):
    rows_n = SAMPLE_BB * dec_len
    assert rows_n == LANES and dec_batch % SAMPLE_BB == 0
    decay, q_dec, k_dec, chunk_dec = _retention_tables(dec_len)
    rot = np.tile(_rotary_tables(PAST_LEN + np.arange(dec_len)), (SAMPLE_BB, 1))
    bd_decay = np.stack([np.kron(np.eye(SAMPLE_BB, dtype=np.float32), d) for d in decay])

    def rows_b(t):
        xp = np if isinstance(t, np.ndarray) else jnp
        return xp.broadcast_to(xp.tile(t, (1, SAMPLE_BB))[:, :, None], (t.shape[0], rows_n, LANES))
    w_tiled = jnp.tile(w_s[:, :dec_len, :dec_len], (1, SAMPLE_BB, SAMPLE_BB))
    causal = jnp.asarray(np.kron(np.eye(SAMPLE_BB), np.tril(np.ones((dec_len, dec_len)))), F32)
    row_spec = pl.BlockSpec((rows_n, D_MODEL), lambda i: (i, 0))
    st_spec = pl.BlockSpec((SAMPLE_BB, RET_HEADS, RET_DK, RET_DV), lambda i: (i, 0, 0, 0))
    cube = (RET_HEADS, rows_n, LANES)
    tab = (rows_n, LANES)
    return pl.pallas_call(
        functools.partial(_ab_sample_kernel, chunk_dec=chunk_dec, dec_len=dec_len),
        grid=(dec_batch // SAMPLE_BB,),
        in_specs=[
            row_spec,
            _const_spec((1, D_MODEL)),
            _const_spec((D_MODEL, AB_IN)),
            _const_spec((AB_OUT, D_MODEL)),
            _const_spec((rows_n, 4 * LANES)),
            _const_spec(cube), _const_spec(cube), _const_spec(cube),
            _const_spec(cube), _const_spec(cube),
            _const_spec((GM_GROUPS, GM_GROUP_DIM)), _const_spec((GM_GROUPS, GM_GROUP_DIM)),
            _const_spec(tab),
            st_spec,
        ],
        out_specs=[
            row_spec,
            st_spec,
            pl.BlockSpec((rows_n, GM_GROUPS * GM_GROUP_DIM), lambda i: (i, 0)),
        ],
        out_shape=[
            jax.ShapeDtypeStruct((dec_batch * dec_len, D_MODEL), F32),
            jax.ShapeDtypeStruct(state.shape, F32),
            jax.ShapeDtypeStruct((dec_batch * dec_len, GM_GROUPS * GM_GROUP_DIM), F32),
        ],
        scratch_shapes=[
            pltpu.VMEM((rows_n, AB_IN), F32),
            pltpu.VMEM((rows_n, AB_OUT), BF16),
        ],
        compiler_params=_params("arbitrary"),
        name="ab_sample",
    )(x, g, w_in, w_o, rot, bd_decay, rows_b(q_dec), rows_b(k_dec),
      w_tiled, rows_b(b_s[:, :dec_len]), ln_g, ln_b, causal, state)


_LOG2E = float(np.log2(np.e))


def _softmax_numerators(s, sink_even, sink_odd):
    s0, s1, s2, s3 = (s[:, i * LANES:(i + 1) * LANES] for i in range(4))
    m_even = jnp.max(jnp.maximum(jnp.maximum(s0, s1), sink_even), axis=-1, keepdims=True)
    m_odd = jnp.max(jnp.maximum(jnp.maximum(s2, s3), sink_odd), axis=-1, keepdims=True)
    p = jnp.concatenate([jnp.exp2(s0 - m_even), jnp.exp2(s1 - m_even),
                         jnp.exp2(s2 - m_odd), jnp.exp2(s3 - m_odd)], axis=1)
    even_lane = lax.broadcasted_iota(jnp.int32, sink_even.shape, 1) < SWA_HD
    sink_term = jnp.where(even_lane, jnp.exp2(sink_even - m_even), jnp.exp2(sink_odd - m_odd))
    return p.astype(BF16), sink_term


def _band_bias(q_rows, first_key_row):
    i = np.arange(q_rows)[:, None]
    j = np.arange(2 * WINDOW)[None, :]
    rel = i + first_key_row - j
    half = np.where((rel >= 0) & (rel <= WINDOW), 0.0, -np.inf).astype(np.float32)
    return np.concatenate([half, half], axis=1)


def _kv_layouts(kcol, vcol):
    even_lane = lax.broadcasted_iota(jnp.int32, kcol.shape, 1) < SWA_HD
    ks = kcol * (SWA_HD ** -0.5 * _LOG2E)
    kr = pltpu.roll(ks, SWA_HD, 1)
    vr = pltpu.roll(vcol, SWA_HD, 1)
    one_even = jnp.where(even_lane, 1.0, 0.0)
    one_odd = 1.0 - one_even
    zero = jnp.zeros_like(kcol)
    out = []
    for kvh in range(SWA_KV_HEADS):
        k_lo, k_hi = (ks, kr) if kvh == 0 else (kr, ks)
        v_lo, v_hi = (vcol, vr) if kvh == 0 else (vr, vcol)
        ktop = jnp.where(even_lane, k_lo, zero).astype(BF16)
        kbot = jnp.where(even_lane, zero, k_hi).astype(BF16)
        vtop = jnp.concatenate([jnp.where(even_lane, v_lo, zero), one_even], axis=1).astype(BF16)
        vbot = jnp.concatenate([jnp.where(even_lane, zero, v_hi), one_odd], axis=1).astype(BF16)
        out.append((ktop, kbot, vtop, vbot))
    return out


def _swa_prompt_kernel(x_ref, xh_ref, g_ref, wqkv_ref, bqkv_ref, wo_ref, bo_ref, sink_ref, bias_ref,
                       o_ref, kc_ref, vc_ref, z_ref, cat_ref):
    si = pl.program_id(1)
    nblk = MIXER_ROWS // WINDOW
    npair = SWA_HEADS // SWA_KV_HEADS // 2
    sub_rows = MIXER_ROWS // MIXER_SPLIT

    x = x_ref[...]
    hn = _rms(x, g_ref[...]).astype(BF16)
    for sub in range(MIXER_SPLIT):
        rows = slice(sub * sub_rows, (sub + 1) * sub_rows)
        z_ref[rows, :] = _dot(hn[rows], wqkv_ref[...]) + bqkv_ref[...]
    hh = _rms(xh_ref[...], g_ref[...]).astype(BF16)
    zh = _dot(hh, wqkv_ref[:, _SK0:]) + bqkv_ref[:, _SK0:]

    kcol = z_ref[:, _SK0:_SK0 + LANES]
    vcol = z_ref[:, _SV0:_SV0 + LANES]
    kc_ref[...] = kcol[MIXER_ROWS - WINDOW:]
    vc_ref[...] = vcol[MIXER_ROWS - WINDOW:]
    layouts = _kv_layouts(jnp.concatenate([zh[:, :LANES], kcol], axis=0),
                          jnp.concatenate([zh[:, LANES:], vcol], axis=0))

    for r in range(nblk):
        rows = slice(r * WINDOW, (r + 1) * WINDOW)
        keys = slice(r * WINDOW, (r + 2) * WINDOW)
        tile = bias_ref[jnp.where(si > 0, 0, 1)] if r == 0 else bias_ref[0]
        bias = jnp.concatenate([tile] * npair, axis=0)
        for kvh, (ktop, kbot, vtop, vbot) in enumerate(layouts):
            c0 = kvh * npair * LANES
            lhs = jnp.concatenate(
                [z_ref[rows, c0 + p * LANES:c0 + (p + 1) * LANES] for p in range(npair)], axis=0).astype(BF16)
            s = _dot_nt(lhs, jnp.concatenate([ktop[keys], kbot[keys]], axis=0)) + bias
            p, sink_term = _softmax_numerators(s, sink_ref[kvh, 0], sink_ref[kvh, 1])
            out = _dot(p, jnp.concatenate([vtop[keys], vbot[keys]], axis=0))
            o = out[:, :LANES] / (out[:, LANES:] + sink_term)
            for p_i in range(npair):
                cat_ref[rows, c0 + p_i * LANES:c0 + (p_i + 1) * LANES] = (
                    o[p_i * WINDOW:(p_i + 1) * WINDOW].astype(BF16))

    o_ref[...] = x + _dot(cat_ref[...], wo_ref[...]) + bo_ref[...]


def _sink_table(sinks, rows_per_pair):
    npair = SWA_HEADS // SWA_KV_HEADS // 2
    s = (sinks.astype(F32) * _LOG2E).reshape(SWA_KV_HEADS, npair, 2).transpose(0, 2, 1)[:, :, :, None, None]
    s = jnp.broadcast_to(s, (SWA_KV_HEADS, 2, npair, rows_per_pair, LANES))
    return s.reshape(SWA_KV_HEADS, 2, npair * rows_per_pair, LANES)


def _swa_prompt(x, g, w_qkv, b_qkv, sinks, w_o, b_o, batch, seq):
    nstep = seq // MIXER_ROWS
    npair = SWA_HEADS // SWA_KV_HEADS // 2
    row_spec = pl.BlockSpec((MIXER_ROWS, D_MODEL), lambda b, s: (b * nstep + s, 0))
    nblk = MIXER_ROWS // WINDOW
    halo_spec = pl.BlockSpec((WINDOW, D_MODEL), lambda b, s: (jnp.maximum((b * nstep + s) * nblk - 1, 0), 0))
    cache_spec = pl.BlockSpec((None, WINDOW, LANES), lambda b, s: (b, 0, 0))
    band = _band_bias(WINDOW, WINDOW)
    no_prev = np.where(np.arange(4 * WINDOW)[None, :] % (2 * WINDOW) >= WINDOW, band, -np.inf).astype(np.float32)
    bias = jnp.asarray(np.stack([band, no_prev]))
    return pl.pallas_call(
        _swa_prompt_kernel,
        grid=(batch, nstep),
        in_specs=[
            row_spec,
            halo_spec,
            _const_spec((1, D_MODEL)),
            _const_spec((D_MODEL, SWA_IN)),
            _const_spec((1, SWA_IN)),
            _const_spec((SWA_OUT, D_MODEL)),
            _const_spec((1, D_MODEL)),
            _const_spec((SWA_KV_HEADS, 2, npair * WINDOW, LANES)),
            _const_spec((2, WINDOW, 4 * WINDOW)),
        ],
        out_specs=[row_spec, cache_spec, cache_spec],
        out_shape=[
            jax.ShapeDtypeStruct((batch * seq, D_MODEL), F32),
            jax.ShapeDtypeStruct((batch, WINDOW, LANES), F32),
            jax.ShapeDtypeStruct((batch, WINDOW, LANES), F32),
        ],
        scratch_shapes=[
            pltpu.VMEM((MIXER_ROWS, SWA_IN), F32),
            pltpu.VMEM((MIXER_ROWS, SWA_OUT), BF16),
        ],
        compiler_params=_params("arbitrary", "arbitrary"),
        name="swa_prompt",
    )(x, x, g, w_qkv, b_qkv, w_o, b_o, _sink_table(sinks, WINDOW), bias)


def _swa_sample_kernel(x_ref, g_ref, wqkv_ref, bqkv_ref, wo_ref, bo_ref, sink_ref, bias_ref, ck_ref, cv_ref,
                       o_ref, nk_ref, nv_ref, z_ref, cat_ref, *, dec_len):
    ncol = SWA_HEADS // 2
    grp = ncol * dec_len
    x = x_ref[...]
    hn = _rms(x, g_ref[...]).astype(BF16)
    z_ref[...] = _dot(hn, wqkv_ref[...]) + bqkv_ref[...]

    bias, sink = bias_ref[...], sink_ref[...]
    lane_lo = lax.broadcasted_iota(jnp.int32, (grp, LANES), 1) < SWA_HD
    kv0 = lax.broadcasted_iota(jnp.int32, (grp, LANES), 0) < grp // 2
    own_half = jnp.where(kv0, 0, 1) == jnp.where(lane_lo, 0, 1)
    pad = jnp.zeros((WINDOW - dec_len, LANES), F32)
    elems = [slice(b * dec_len, (b + 1) * dec_len) for b in range(SAMPLE_BB)]
    values, scores = [], []
    for b, rows in enumerate(elems):
        knew = z_ref[rows, _SK0:_SK0 + LANES]
        vnew = z_ref[rows, _SV0:_SV0 + LANES]
        kold, vold = ck_ref[b], cv_ref[b]
        nk_ref[b, 0:WINDOW - dec_len, :] = kold[dec_len:]
        nk_ref[b, WINDOW - dec_len:, :] = knew
        nv_ref[b, 0:WINDOW - dec_len, :] = vold[dec_len:]
        nv_ref[b, WINDOW - dec_len:, :] = vnew
        kall = jnp.concatenate([kold, knew, pad], axis=0).astype(BF16)
        values.append(jnp.concatenate([vold, vnew, pad], axis=0).astype(BF16))

        q = jnp.concatenate([z_ref[rows, j * LANES:(j + 1) * LANES] for j in range(ncol)], axis=0)
        q = q * (SWA_HD ** -0.5 * _LOG2E)
        q_swapped = pltpu.roll(q, SWA_HD, 1)
        q_even = jnp.where(own_half, jnp.where(kv0, q, q_swapped), 0.0)
        q_odd = jnp.where(own_half, jnp.where(kv0, q_swapped, q), 0.0)
        lhs = jnp.concatenate([q_even, q_odd], axis=0).astype(BF16)
        scores.append(_dot_nt(lhs, kall) + bias)

    probs, dens = [], []
    for s in scores:
        m = jnp.max(jnp.maximum(jnp.maximum(s[:, :LANES], s[:, LANES:]), sink), axis=-1, keepdims=True)
        p = jnp.exp2(s - m)
        dens.append(jnp.sum(p, axis=-1, keepdims=True) + jnp.exp2(sink - m))
        probs.append(p.astype(BF16))

    outs = [_dot(p, v) for p, v in zip(probs, values)]

    for rows, out, den in zip(elems, outs, dens):
        o = out / den
        o_swapped = pltpu.roll(o, SWA_HD, 1)
        cols = jnp.where(lane_lo,
                         jnp.where(kv0, o[:grp], o_swapped[:grp]),
                         jnp.where(kv0, o_swapped[grp:], o[grp:]))
        for j in range(ncol):
            cat_ref[rows, j * LANES:(j + 1) * LANES] = cols[j * dec_len:(j + 1) * dec_len]

    o_ref[...] = x + _dot(cat_ref[...].astype(BF16), wo_ref[...]) + bo_ref[...]


def _swa_sample(x, g, w_qkv, b_qkv, sinks, w_o, b_o, cache_k, cache_v, dec_batch, dec_len):
    rows_n = SAMPLE_BB * dec_len
    ncol = SWA_HEADS // 2
    score_rows = 2 * ncol * dec_len
    sink = jnp.broadcast_to((sinks.astype(F32) * _LOG2E).reshape(ncol, 2).T[:, :, None, None],
                            (2, ncol, dec_len, LANES)).reshape(score_rows, LANES)
    bias = jnp.asarray(np.tile(_band_bias(dec_len, WINDOW)[:, :2 * WINDOW], (2 * ncol, 1)))
    row_spec = pl.BlockSpec((rows_n, D_MODEL), lambda i: (i, 0))
    cache_spec = pl.BlockSpec((SAMPLE_BB, WINDOW, LANES), lambda i: (i, 0, 0))
    return pl.pallas_call(
        functools.partial(_swa_sample_kernel, dec_len=dec_len),
        grid=(dec_batch // SAMPLE_BB,),
        in_specs=[
            row_spec,
            _const_spec((1, D_MODEL)),
            _const_spec((D_MODEL, SWA_IN)),
            _const_spec((1, SWA_IN)),
            _const_spec((SWA_OUT, D_MODEL)),
            _const_spec((1, D_MODEL)),
            _const_spec((score_rows, LANES)),
            _const_spec((score_rows, 2 * WINDOW)),
            cache_spec, cache_spec,
        ],
        out_specs=[row_spec, cache_spec, cache_spec],
        out_shape=[
            jax.ShapeDtypeStruct((dec_batch * dec_len, D_MODEL), F32),
            jax.ShapeDtypeStruct((dec_batch, WINDOW, LANES), F32),
            jax.ShapeDtypeStruct((dec_batch, WINDOW, LANES), F32),
        ],
        scratch_shapes=[
            pltpu.VMEM((rows_n, SWA_IN), F32),
            pltpu.VMEM((rows_n, SWA_OUT), F32),
        ],
        compiler_params=_params("arbitrary"),
        name="swa_sample",
    )(x, g, w_qkv, b_qkv, w_o, b_o, sink, bias, cache_k, cache_v)


def kernel(x_prompt, x_sample, state_ret, cache_swa_k, cache_swa_v, norm_mix, norm_mlp, norm_final, ab_w_in, ab_w_s, ab_b_s, ab_ln_g, ab_ln_b, ab_w_o, swa_w_qkv, swa_b_qkv, swa_sinks, swa_w_o, swa_b_o, mlp_w_up, mlp_w_down):
    batch, seq, _ = x_prompt.shape
    dec_batch, dec_len, _ = x_sample.shape
    assert seq % MIXER_ROWS == 0 and seq % RET_CHUNK == 0
    kv_lanes = SWA_KV_HEADS * SWA_HD

    hp = x_prompt.reshape(batch * seq, D_MODEL)
    hs = x_sample.reshape(dec_batch * dec_len, D_MODEL)
    row = lambda v: v.reshape(1, -1)
    gfin = row(norm_final)

    w_in, w_o = ab_w_in[0].astype(BF16), ab_w_o[0].astype(BF16)
    hp, ret_p, (w_up, w_dn) = _ab_prompt(hp, row(norm_mix[0]), w_in, w_o, ab_w_s[0], ab_b_s[0], ab_ln_g[0],
                                         ab_ln_b[0], batch, seq, cast=[(mlp_w_up, 0), (mlp_w_down, 0)])
    hs, ret_s, gm_s = _ab_sample(hs, row(norm_mix[0]), w_in, w_o, ab_w_s[0], ab_b_s[0], ab_ln_g[0], ab_ln_b[0],
                                 state_ret[0], dec_batch, dec_len)
    hp, (w_qkv, w_so, w_up1, w_dn1) = _mlp(hp, row(norm_mlp[0]), w_up, w_dn, gfin, final_norm=False,
                                           name="mlp0_prompt",
                                           cast=[(swa_w_qkv, 0), (swa_w_o, 0), (mlp_w_up, 1), (mlp_w_down, 1)])
    hs, _ = _mlp(hs, row(norm_mlp[0]), w_up, w_dn, gfin, final_norm=False, name="mlp0_sample")

    hp, kp, vp = _swa_prompt(hp, row(norm_mix[1]), w_qkv, row(swa_b_qkv[0]), swa_sinks[0], w_so, row(swa_b_o[0]),
                             batch, seq)
    hs, ks, vs = _swa_sample(hs, row(norm_mix[1]), w_qkv, row(swa_b_qkv[0]), swa_sinks[0], w_so, row(swa_b_o[0]),
                             cache_swa_k[0].reshape(dec_batch, WINDOW, kv_lanes),
                             cache_swa_v[0].reshape(dec_batch, WINDOW, kv_lanes), dec_batch, dec_len)
    hp, _ = _mlp(hp, row(norm_mlp[1]), w_up1, w_dn1, gfin, final_norm=True, name="mlp1_prompt")
    hs, _ = _mlp(hs, row(norm_mlp[1]), w_up1, w_dn1, gfin, final_norm=True, name="mlp1_sample")

    cache_shape = lambda n: (1, n, WINDOW, SWA_KV_HEADS, SWA_HD)
    return (hp.reshape(batch, seq, D_MODEL),
            hs.reshape(dec_batch, dec_len, D_MODEL),
            ret_p[None],
            ret_s[None],
            gm_s.reshape(1, dec_batch, dec_len, GM_GROUPS * GM_GROUP_DIM),
            kp.reshape(cache_shape(batch)),
            vp.reshape(cache_shape(batch)),
            ks.reshape(cache_shape(dec_batch)),
            vs.reshape(cache_shape(dec_batch)))
```

```python
import functools

import numpy as np
import jax
import jax.numpy as jnp
from jax import lax
from jax.experimental import pallas as pl
from jax.experimental.pallas import tpu as pltpu

F32 = jnp.float32
BF16 = jnp.bfloat16

D_MODEL = 1024
PAST_LEN = 16384
RMS_EPS = 1e-6
LN_EPS = 1e-5
RET_HEADS = 4
RET_DK = 128
RET_DV = 128
RET_CHUNK = 128
ROPE_BASE = 10000.0
GM_GROUPS = 4
GM_GROUP_DIM = 128
GM_CHUNK = 128
AB_IN = 3072
AB_OUT = 1024
SWA_HEADS = 16
SWA_KV_HEADS = 2
SWA_HD = 64
WINDOW = 128
SWA_IN = 1280
SWA_OUT = 1024
D_FF = 4096

_Q0, _K0, _V0, _G0, _U0, _GV0 = 0, 512, 1024, 1536, 2048, 2560
_SK0, _SV0 = 1024, 1152

LANES = 128
MIXER_ROWS = 1024
MIXER_SPLIT = 4
MLP_ROWS = 1024
MLP_SPLIT = 4
FF_CHUNK = 1024
SAMPLE_BB = 16
VMEM_LIMIT = 56 * 1024 * 1024


def _rms(x, g):
    return x * lax.rsqrt(jnp.mean(x * x, axis=-1, keepdims=True) + RMS_EPS) * g


def _gelu(x):
    return 0.5 * x * (1.0 + lax.erf(x * np.float32(np.sqrt(0.5))))


def _silu(x):
    return x / (1.0 + jnp.exp(-x))


def _dot(a, b):
    return jnp.dot(a, b, preferred_element_type=F32)


def _dot_nt(a, b):
    return lax.dot_general(a, b, (((1,), (1,)), ((), ())), preferred_element_type=F32)


def _dot_tn(a, b):
    return lax.dot_general(a, b, (((0,), (0,)), ((), ())), preferred_element_type=F32)


def _const_spec(shape):
    zeros = (0,) * len(shape)
    return pl.BlockSpec(shape, lambda *_: zeros, pipeline_mode=pl.Buffered(1))


def _params(*sem):
    return pltpu.CompilerParams(dimension_semantics=sem, vmem_limit_bytes=VMEM_LIMIT)


def _cast_specs(weights, steps, flat_step):
    def src_map(layer):
        return lambda *g: (layer, flat_step(*g), 0)

    in_specs, out_specs, out_shapes = [], [], []
    for w, layer in weights:
        _, rows, cols = w.shape
        slab = rows // steps
        assert slab * steps == rows and slab % 16 == 0, (w.shape, steps)
        in_specs.append(pl.BlockSpec((None, slab, cols), src_map(layer)))
        out_specs.append(pl.BlockSpec((slab, cols), lambda *g: (flat_step(*g), 0)))
        out_shapes.append(jax.ShapeDtypeStruct((rows, cols), BF16))
    return in_specs, out_specs, out_shapes


def _cast_slabs(src_refs, dst_refs):
    for src, dst in zip(src_refs, dst_refs):
        dst[...] = src[...].astype(BF16)


def _mlp_kernel(x_ref, g_ref, wup_ref, wdn_ref, gf_ref, *rest, final_norm, n_cast):
    cast_src, o_ref, cast_dst = rest[:n_cast], rest[n_cast], rest[n_cast + 1:]
    _cast_slabs(cast_src, cast_dst)
    rows = x_ref.shape[0] // MLP_SPLIT
    groups = [slice(i * rows, (i + 1) * rows) for i in range(MLP_SPLIT)]
    xs = [x_ref[r, :] for r in groups]
    hns = [_rms(x, g_ref[...]).astype(BF16) for x in xs]
    accs = [jnp.zeros_like(x) for x in xs]
    for c in range(D_FF // FF_CHUNK):
        cols = slice(c * FF_CHUNK, (c + 1) * FF_CHUNK)
        for i in range(MLP_SPLIT):
            h = _dot(hns[i], wup_ref[:, cols])
            h = jnp.square(jnp.maximum(h, 0.0)).astype(BF16)
            accs[i] = accs[i] + _dot(h, wdn_ref[cols, :])
    for i, r in enumerate(groups):
        y = xs[i] + accs[i]
        if final_norm:
            y = _rms(y, gf_ref[...])
        o_ref[r, :] = y


def _mlp(x, g, wup, wdn, gf, *, final_norm, name, cast=()):
    rows = x.shape[0]
    tm = min(MLP_ROWS, rows)
    steps = rows // tm
    row_spec = pl.BlockSpec((tm, D_MODEL), lambda i: (i, 0))
    cast_in, cast_out, cast_shapes = _cast_specs(cast, steps, lambda i: i)
    out, *cast_w = pl.pallas_call(
        functools.partial(_mlp_kernel, final_norm=final_norm, n_cast=len(cast)),
        grid=(steps,),
        in_specs=[
            row_spec,
            _const_spec((1, D_MODEL)),
            _const_spec((D_MODEL, D_FF)),
            _const_spec((D_FF, D_MODEL)),
            _const_spec((1, D_MODEL)),
            *cast_in,
        ],
        out_specs=[row_spec, *cast_out],
        out_shape=[jax.ShapeDtypeStruct((rows, D_MODEL), F32), *cast_shapes],
        compiler_params=_params("arbitrary"),
        name=name,
    )(x, g, wup, wdn, gf, *(w for w, _ in cast))
    return out, cast_w


def _rotary(x, cos2, sin2):
    return x * cos2 + pltpu.roll(x, 64, 1) * sin2


def _retention_tables(chunk):
    log_g = np.log1p(-np.exp2(-5.0 - np.arange(RET_HEADS, dtype=np.float64)))
    idx = np.arange(chunk, dtype=np.float64)
    diff = idx[:, None] - idx[None, :]
    decay = np.where(diff[None] >= 0, np.exp(log_g[:, None, None] * np.maximum(diff, 0.0)[None]), 0.0)
    q_dec = np.exp(log_g[:, None] * (idx[None, :] + 1.0))
    k_dec = np.exp(log_g[:, None] * (chunk - 1.0 - idx[None, :]))
    chunk_dec = tuple(float(v) for v in np.exp(log_g * chunk))
    f32 = lambda t: t.astype(np.float32)
    return f32(decay), f32(q_dec), f32(k_dec), chunk_dec


def _rotary_tables(pos):
    half = RET_DK // 2
    inv = ROPE_BASE ** (-np.arange(half, dtype=np.float64) / half)
    ang = np.asarray(pos, np.float64)[:, None] * inv[None, :]
    cos, sin = np.cos(ang), np.sin(ang)
    cos2 = np.concatenate([cos, cos], axis=-1)
    sin2 = np.concatenate([-sin, sin], axis=-1)
    scale = RET_DK ** -0.5
    return np.concatenate([cos2 * scale, sin2 * scale, cos2, sin2], axis=-1).astype(np.float32)


def _gmlp_branch(z_ref, rows, lng_ref, lnb_ref, gi):
    u = _gelu(z_ref[rows, _U0 + gi * LANES:_U0 + (gi + 1) * LANES])
    gv = _gelu(z_ref[rows, _GV0 + gi * LANES:_GV0 + (gi + 1) * LANES])
    mu = jnp.mean(gv, axis=-1, keepdims=True)
    cen = gv - mu
    var = jnp.mean(cen * cen, axis=-1, keepdims=True)
    gvn = cen * lax.rsqrt(var + LN_EPS) * lng_ref[gi:gi + 1, :] + lnb_ref[gi:gi + 1, :]
    return u, gvn


def _ab_prompt_kernel(x_ref, g_ref, win_ref, wo_ref, rot_ref,
                      dec_ref, qd_ref, kd_ref, ws_ref, bs_ref, lng_ref, lnb_ref,
                      *rest, chunk_dec, n_cast):
    cast_src, (o_ref, st_ref), cast_dst = rest[:n_cast], rest[n_cast:n_cast + 2], rest[n_cast + 2:2 * n_cast + 2]
    scratch = rest[2 * n_cast + 2:]
    _cast_slabs(cast_src, cast_dst)
    si = pl.program_id(1)
    z_refs, cat_refs, s_ref = scratch[:MIXER_SPLIT], scratch[MIXER_SPLIT:2 * MIXER_SPLIT], scratch[-1]
    sub_rows = MIXER_ROWS // MIXER_SPLIT
    nchunk = sub_rows // RET_CHUNK

    @pl.when(si == 0)
    def _():
        s_ref[...] = jnp.zeros_like(s_ref)

    ri = lax.broadcasted_iota(jnp.int32, (GM_CHUNK, GM_CHUNK), 0)
    ci = lax.broadcasted_iota(jnp.int32, (GM_CHUNK, GM_CHUNK), 1)
    w_spatial = [jnp.where(ci <= ri, ws_ref[gi], 0.0).astype(BF16) for gi in range(GM_GROUPS)]
    tile_rows = lambda t: jnp.concatenate([t] * nchunk, axis=0)
    states = [s_ref[h] for h in range(RET_HEADS)]

    for sub in range(MIXER_SPLIT):
        rows = slice(sub * sub_rows, (sub + 1) * sub_rows)
        chunks = [slice(c * RET_CHUNK, (c + 1) * RET_CHUNK) for c in range(nchunk)]
        z_ref, cat_ref = z_refs[sub], cat_refs[sub]
        x = x_ref[rows, :]
        hn = _rms(x, g_ref[...]).astype(BF16)
        z_ref[:, :_U0] = _dot(hn, win_ref[:, :_U0])
        z_ref[:, _U0:] = _dot(hn, win_ref[:, _U0:])

        cq, sq, ck, sk = (rot_ref[rows, i * LANES:(i + 1) * LANES] for i in range(4))
        stage1 = []
        for h in range(RET_HEADS):
            q = _rotary(z_ref[:, _Q0 + h * LANES:_Q0 + (h + 1) * LANES], cq, sq)
            k = _rotary(z_ref[:, _K0 + h * LANES:_K0 + (h + 1) * LANES], ck, sk)
            v = z_ref[:, _V0 + h * LANES:_V0 + (h + 1) * LANES].astype(BF16)
            qb = q.astype(BF16)
            qdb = (q * tile_rows(qd_ref[h])).astype(BF16)
            local = [slice(c * RET_CHUNK, (c + 1) * RET_CHUNK) for c in range(nchunk)]
            kts = [k[r].T for r in local]
            scores = [_dot(qb[r], kt.astype(BF16)) for r, kt in zip(local, kts)]
            kvs = [_dot((kt * kd_ref[h]).astype(BF16), v[r]) for r, kt in zip(local, kts)]
            before = []
            for kv in kvs:
                before.append(states[h].astype(BF16))
                states[h] = states[h] * chunk_dec[h] + kv
            stage1.append((v, qdb, scores, before, local))
        for h, (v, qdb, scores, before, local) in enumerate(stage1):
            lanes = slice(h * LANES, (h + 1) * LANES)
            dec = dec_ref[h]
            for c, r in enumerate(local):
                lhs = jnp.concatenate([(scores[c] * dec).astype(BF16), qdb[r]], axis=1)
                rhs = jnp.concatenate([v[r], before[c]], axis=0)
                o = _dot(lhs, rhs)
                o = o * lax.rsqrt(jnp.mean(o * o, axis=-1, keepdims=True) + RMS_EPS)
                gate = _silu(z_ref[chunks[c], _G0 + h * LANES:_G0 + (h + 1) * LANES])
                cat_ref[chunks[c], lanes] = (gate * o).astype(BF16)

        for gi in range(GM_GROUPS):
            u, gvn = _gmlp_branch(z_ref, slice(None), lng_ref, lnb_ref, gi)
            gvb = gvn.astype(BF16)
            rhs = jnp.concatenate([gvb[c * GM_CHUNK:(c + 1) * GM_CHUNK] for c in range(nchunk)], axis=1)
            mixed = _dot(w_spatial[gi], rhs)
            bias = bs_ref[gi]
            for c in range(nchunk):
                local_rows = slice(c * GM_CHUNK, (c + 1) * GM_CHUNK)
                m = mixed[:, c * GM_CHUNK:(c + 1) * GM_CHUNK] + bias
                cat_ref[chunks[c], 512 + gi * LANES:512 + (gi + 1) * LANES] = (u[local_rows] * m).astype(BF16)

        o_ref[rows, :] = x + _dot(cat_ref[...], wo_ref[...])

    for h in range(RET_HEADS):
        s_ref[h] = states[h]

    @pl.when(si == pl.num_programs(1) - 1)
    def _():
        st_ref[...] = s_ref[...]


def _ab_prompt(x, g, w_in, w_o, w_s, b_s, ln_g, ln_b, batch, seq, cast=()):
    nstep = seq // MIXER_ROWS
    cast_in, cast_out, cast_shapes = _cast_specs(cast, batch * nstep, lambda b, s: b * nstep + s)
    decay, q_dec, k_dec, chunk_dec = _retention_tables(RET_CHUNK)
    rot = _rotary_tables(np.arange(seq))
    lane_b = lambda t: (np if isinstance(t, np.ndarray) else jnp).broadcast_to(t[:, :, None], t.shape + (LANES,))
    row_spec = pl.BlockSpec((MIXER_ROWS, D_MODEL), lambda b, s: (b * nstep + s, 0))
    rot_spec = pl.BlockSpec((MIXER_ROWS, 4 * LANES), lambda b, s: (s, 0))
    cube = (RET_HEADS, RET_CHUNK, LANES)
    out, state, *cast_w = pl.pallas_call(
        functools.partial(_ab_prompt_kernel, chunk_dec=chunk_dec, n_cast=len(cast)),
        grid=(batch, nstep),
        in_specs=[
            row_spec,
            _const_spec((1, D_MODEL)),
            _const_spec((D_MODEL, AB_IN)),
            _const_spec((AB_OUT, D_MODEL)),
            rot_spec,
            _const_spec(cube), _const_spec(cube), _const_spec(cube),
            _const_spec(cube), _const_spec(cube),
            _const_spec((GM_GROUPS, GM_GROUP_DIM)), _const_spec((GM_GROUPS, GM_GROUP_DIM)),
            *cast_in,
        ],
        out_specs=[
            row_spec,
            pl.BlockSpec((None, RET_HEADS, RET_DK, RET_DV), lambda b, s: (b, 0, 0, 0)),
            *cast_out,
        ],
        out_shape=[
            jax.ShapeDtypeStruct((batch * seq, D_MODEL), F32),
            jax.ShapeDtypeStruct((batch, RET_HEADS, RET_DK, RET_DV), F32),
            *cast_shapes,
        ],
        scratch_shapes=(
            [pltpu.VMEM((MIXER_ROWS // MIXER_SPLIT, AB_IN), F32)] * MIXER_SPLIT
            + [pltpu.VMEM((MIXER_ROWS // MIXER_SPLIT, AB_OUT), BF16)] * MIXER_SPLIT
            + [pltpu.VMEM((RET_HEADS, RET_DK, RET_DV), F32)]),
        compiler_params=_params("arbitrary", "arbitrary"),
        name="ab_prompt",
    )(x, g, w_in, w_o, rot, decay, lane_b(q_dec), np.broadcast_to(k_dec[:, None, :], cube),
      w_s, lane_b(b_s), ln_g, ln_b, *(w for w, _ in cast))
    return out, state, cast_w


def _ab_sample_kernel(x_ref, g_ref, win_ref, wo_ref, rot_ref,
                      dec_ref, qd_ref, kd_ref, ws_ref, bs_ref, lng_ref, lnb_ref, causal_ref, st_ref,
                      o_ref, sto_ref, gvn_ref, z_ref, cat_ref, *, chunk_dec, dec_len):
    rows_n = SAMPLE_BB * dec_len
    x = x_ref[...]
    hn = _rms(x, g_ref[...]).astype(BF16)
    z_ref[...] = _dot(hn, win_ref[...])

    cq, sq, ck, sk = (rot_ref[:, i * LANES:(i + 1) * LANES] for i in range(4))
    token = lax.broadcasted_iota(jnp.int32, (rows_n, rows_n), 1)
    for h in range(RET_HEADS):
        lanes = slice(h * LANES, (h + 1) * LANES)
        q = _rotary(z_ref[:, _Q0 + h * LANES:_Q0 + (h + 1) * LANES], cq, sq)
        k = _rotary(z_ref[:, _K0 + h * LANES:_K0 + (h + 1) * LANES], ck, sk)
        v = z_ref[:, _V0 + h * LANES:_V0 + (h + 1) * LANES].astype(BF16)
        scores = _dot_nt(q.astype(BF16), k.astype(BF16)) * dec_ref[h]
        intra = _dot(scores.astype(BF16), v)
        qs = q * qd_ref[h]
        kst = (k * kd_ref[h]).T
        kst_all = jnp.concatenate(
            [jnp.where((token >= b * dec_len) & (token < (b + 1) * dec_len), kst, 0.0).astype(BF16)
             for b in range(SAMPLE_BB)], axis=0)
        kv_all = _dot(kst_all, v)
        cross = []
        for b in range(SAMPLE_BB):
            rows = slice(b * dec_len, (b + 1) * dec_len)
            state = st_ref[b, h]
            cross.append(_dot(qs[rows].astype(BF16), state.astype(BF16)))
            sto_ref[b, h] = state * chunk_dec[h] + kv_all[b * RET_DK:(b + 1) * RET_DK]
        o = intra + jnp.concatenate(cross, axis=0)
        o = o * lax.rsqrt(jnp.mean(o * o, axis=-1, keepdims=True) + RMS_EPS)
        gate = _silu(z_ref[:, _G0 + h * LANES:_G0 + (h + 1) * LANES])
        cat_ref[:, lanes] = (gate * o).astype(BF16)

    causal = causal_ref[...] > 0.0
    for gi in range(GM_GROUPS):
        u, gvn = _gmlp_branch(z_ref, slice(None), lng_ref, lnb_ref, gi)
        gvn_ref[:, gi * LANES:(gi + 1) * LANES] = gvn
        w = jnp.where(causal, ws_ref[gi], 0.0).astype(BF16)
        mixed = _dot(w, gvn.astype(BF16)) + bs_ref[gi]
        cat_ref[:, 512 + gi * LANES:512 + (gi + 1) * LANES] = (u * mixed).astype(BF16)

    o_ref[...] = x + _dot(cat_ref[...], wo_ref[...])


def _ab_sample(x, g, w_in, w_o, w_s, b_s, ln_g, ln_b, state, dec_batch, dec_len):
    rows_n = SAMPLE_BB * dec_len
    assert rows_n == LANES and dec_batch % SAMPLE_BB == 0
    decay, q_dec, k_dec, chunk_dec = _retention_tables(dec_len)
    rot = np.tile(_rotary_tables(PAST_LEN + np.arange(dec_len)), (SAMPLE_BB, 1))
    bd_decay = np.stack([np.kron(np.eye(SAMPLE_BB, dtype=np.float32), d) for d in decay])

    def rows_b(t):
        xp = np if isinstance(t, np.ndarray) else jnp
        return xp.broadcast_to(xp.tile(t, (1, SAMPLE_BB))[:, :, None], (t.shape[0], rows_n, LANES))
    w_tiled = jnp.tile(w_s[:, :dec_len, :dec_len], (1, SAMPLE_BB, SAMPLE_BB))
    causal = jnp.asarray(np.kron(np.eye(SAMPLE_BB), np.tril(np.ones((dec_len, dec_len)))), F32)
    row_spec = pl.BlockSpec((rows_n, D_MODEL), lambda i: (i, 0))
    st_spec = pl.BlockSpec((SAMPLE_BB, RET_HEADS, RET_DK, RET_DV), lambda i: (i, 0, 0, 0))
    cube = (RET_HEADS, rows_n, LANES)
    tab = (rows_n, LANES)
    return pl.pallas_call(
        functools.partial(_ab_sample_kernel, chunk_dec=chunk_dec, dec_len=dec_len),
        grid=(dec_batch // SAMPLE_BB,),
        in_specs=[
            row_spec,
            _const_spec((1, D_MODEL)),
            _const_spec((D_MODEL, AB_IN)),
            _const_spec((AB_OUT, D_MODEL)),
            _const_spec((rows_n, 4 * LANES)),
            _const_spec(cube), _const_spec(cube), _const_spec(cube),
            _const_spec(cube), _const_spec(cube),
            _const_spec((GM_GROUPS, GM_GROUP_DIM)), _const_spec((GM_GROUPS, GM_GROUP_DIM)),
            _const_spec(tab),
            st_spec,
        ],
        out_specs=[
            row_spec,
            st_spec,
            pl.BlockSpec((rows_n, GM_GROUPS * GM_GROUP_DIM), lambda i: (i, 0)),
        ],
        out_shape=[
            jax.ShapeDtypeStruct((dec_batch * dec_len, D_MODEL), F32),
            jax.ShapeDtypeStruct(state.shape, F32),
            jax.ShapeDtypeStruct((dec_batch * dec_len, GM_GROUPS * GM_GROUP_DIM), F32),
        ],
        scratch_shapes=[
            pltpu.VMEM((rows_n, AB_IN), F32),
            pltpu.VMEM((rows_n, AB_OUT), BF16),
        ],
        compiler_params=_params("arbitrary"),
        name="ab_sample",
    )(x, g, w_in, w_o, rot, bd_decay, rows_b(q_dec), rows_b(k_dec),
      w_tiled, rows_b(b_s[:, :dec_len]), ln_g, ln_b, causal, state)


_LOG2E = float(np.log2(np.e))


def _softmax_numerators(s, sink_even, sink_odd):
    s0, s1, s2, s3 = (s[:, i * LANES:(i + 1) * LANES] for i in range(4))
    m_even = jnp.max(jnp.maximum(jnp.maximum(s0, s1), sink_even), axis=-1, keepdims=True)
    m_odd = jnp.max(jnp.maximum(jnp.maximum(s2, s3), sink_odd), axis=-1, keepdims=True)
    p = jnp.concatenate([jnp.exp2(s0 - m_even), jnp.exp2(s1 - m_even),
                         jnp.exp2(s2 - m_odd), jnp.exp2(s3 - m_odd)], axis=1)
    even_lane = lax.broadcasted_iota(jnp.int32, sink_even.shape, 1) < SWA_HD
    sink_term = jnp.where(even_lane, jnp.exp2(sink_even - m_even), jnp.exp2(sink_odd - m_odd))
    return p.astype(BF16), sink_term


def _band_bias(q_rows, first_key_row):
    i = np.arange(q_rows)[:, None]
    j = np.arange(2 * WINDOW)[None, :]
    rel = i + first_key_row - j
    half = np.where((rel >= 0) & (rel <= WINDOW), 0.0, -np.inf).astype(np.float32)
    return np.concatenate([half, half], axis=1)


def _kv_layouts(kcol, vcol):
    even_lane = lax.broadcasted_iota(jnp.int32, kcol.shape, 1) < SWA_HD
    ks = kcol * (SWA_HD ** -0.5 * _LOG2E)
    kr = pltpu.roll(ks, SWA_HD, 1)
    vr = pltpu.roll(vcol, SWA_HD, 1)
    one_even = jnp.where(even_lane, 1.0, 0.0)
    one_odd = 1.0 - one_even
    zero = jnp.zeros_like(kcol)
    out = []
    for kvh in range(SWA_KV_HEADS):
        k_lo, k_hi = (ks, kr) if kvh == 0 else (kr, ks)
        v_lo, v_hi = (vcol, vr) if kvh == 0 else (vr, vcol)
        ktop = jnp.where(even_lane, k_lo, zero).astype(BF16)
        kbot = jnp.where(even_lane, zero, k_hi).astype(BF16)
        vtop = jnp.concatenate([jnp.where(even_lane, v_lo, zero), one_even], axis=1).astype(BF16)
        vbot = jnp.concatenate([jnp.where(even_lane, zero, v_hi), one_odd], axis=1).astype(BF16)
        out.append((ktop, kbot, vtop, vbot))
    return out


def _swa_prompt_kernel(x_ref, xh_ref, g_ref, wqkv_ref, bqkv_ref, wo_ref, bo_ref, sink_ref, bias_ref,
                       o_ref, kc_ref, vc_ref, z_ref, cat_ref):
    si = pl.program_id(1)
    nblk = MIXER_ROWS // WINDOW
    npair = SWA_HEADS // SWA_KV_HEADS // 2
    sub_rows = MIXER_ROWS // MIXER_SPLIT

    x = x_ref[...]
    hn = _rms(x, g_ref[...]).astype(BF16)
    for sub in range(MIXER_SPLIT):
        rows = slice(sub * sub_rows, (sub + 1) * sub_rows)
        z_ref[rows, :] = _dot(hn[rows], wqkv_ref[...]) + bqkv_ref[...]
    hh = _rms(xh_ref[...], g_ref[...]).astype(BF16)
    zh = _dot(hh, wqkv_ref[:, _SK0:]) + bqkv_ref[:, _SK0:]

    kcol = z_ref[:, _SK0:_SK0 + LANES]
    vcol = z_ref[:, _SV0:_SV0 + LANES]
    kc_ref[...] = kcol[MIXER_ROWS - WINDOW:]
    vc_ref[...] = vcol[MIXER_ROWS - WINDOW:]
    layouts = _kv_layouts(jnp.concatenate([zh[:, :LANES], kcol], axis=0),
                          jnp.concatenate([zh[:, LANES:], vcol], axis=0))

    sink_rows = lambda tab: jnp.concatenate(
        [tab[p * 8:(p + 1) * 8] for p in range(npair) for _ in range(WINDOW // 8)], axis=0)
    sinks = [[sink_rows(sink_ref[kvh, par]) for par in range(2)] for kvh in range(SWA_KV_HEADS)]

    for r in range(nblk):
        rows = slice(r * WINDOW, (r + 1) * WINDOW)
        keys = slice(r * WINDOW, (r + 2) * WINDOW)
        tile = bias_ref[jnp.where(si > 0, 0, 1)] if r == 0 else bias_ref[0]
        bias = jnp.concatenate([tile] * npair, axis=0)
        for kvh, (ktop, kbot, vtop, vbot) in enumerate(layouts):
            c0 = kvh * npair * LANES
            lhs = jnp.concatenate(
                [z_ref[rows, c0 + p * LANES:c0 + (p + 1) * LANES] for p in range(npair)], axis=0).astype(BF16)
            s = _dot_nt(lhs, jnp.concatenate([ktop[keys], kbot[keys]], axis=0)) + bias
            p, sink_term = _softmax_numerators(s, sinks[kvh][0], sinks[kvh][1])
            out = _dot(p, jnp.concatenate([vtop[keys], vbot[keys]], axis=0))
            o = out[:, :LANES] / (out[:, LANES:] + sink_term)
            for p_i in range(npair):
                cat_ref[rows, c0 + p_i * LANES:c0 + (p_i + 1) * LANES] = (
                    o[p_i * WINDOW:(p_i + 1) * WINDOW].astype(BF16))

    o_ref[...] = x + _dot(cat_ref[...], wo_ref[...]) + bo_ref[...]


def _sink_table(sinks, rows_per_pair):
    npair = SWA_HEADS // SWA_KV_HEADS // 2
    s = (sinks.astype(F32) * _LOG2E).reshape(SWA_KV_HEADS, npair, 2).transpose(0, 2, 1)[:, :, :, None, None]
    s = jnp.broadcast_to(s, (SWA_KV_HEADS, 2, npair, rows_per_pair, LANES))
    return s.reshape(SWA_KV_HEADS, 2, npair * rows_per_pair, LANES)


def _swa_prompt(x, g, w_qkv, b_qkv, sinks, w_o, b_o, batch, seq):
    nstep = seq // MIXER_ROWS
    npair = SWA_HEADS // SWA_KV_HEADS // 2
    row_spec = pl.BlockSpec((MIXER_ROWS, D_MODEL), lambda b, s: (b * nstep + s, 0))
    nblk = MIXER_ROWS // WINDOW
    halo_spec = pl.BlockSpec((WINDOW, D_MODEL), lambda b, s: (jnp.maximum((b * nstep + s) * nblk - 1, 0), 0))
    cache_spec = pl.BlockSpec((None, WINDOW, LANES), lambda b, s: (b, 0, 0))
    band = _band_bias(WINDOW, WINDOW)
    no_prev = np.where(np.arange(4 * WINDOW)[None, :] % (2 * WINDOW) >= WINDOW, band, -np.inf).astype(np.float32)
    bias = jnp.asarray(np.stack([band, no_prev]))
    return pl.pallas_call(
        _swa_prompt_kernel,
        grid=(batch, nstep),
        in_specs=[
            row_spec,
            halo_spec,
            _const_spec((1, D_MODEL)),
            _const_spec((D_MODEL, SWA_IN)),
            _const_spec((1, SWA_IN)),
            _const_spec((SWA_OUT, D_MODEL)),
            _const_spec((1, D_MODEL)),
            _const_spec((SWA_KV_HEADS, 2, npair * 8, LANES)),
            _const_spec((2, WINDOW, 4 * WINDOW)),
        ],
        out_specs=[row_spec, cache_spec, cache_spec],
        out_shape=[
            jax.ShapeDtypeStruct((batch * seq, D_MODEL), F32),
            jax.ShapeDtypeStruct((batch, WINDOW, LANES), F32),
            jax.ShapeDtypeStruct((batch, WINDOW, LANES), F32),
        ],
        scratch_shapes=[
            pltpu.VMEM((MIXER_ROWS, SWA_IN), F32),
            pltpu.VMEM((MIXER_ROWS, SWA_OUT), BF16),
        ],
        compiler_params=_params("arbitrary", "arbitrary"),
        name="swa_prompt",
    )(x, x, g, w_qkv, b_qkv, w_o, b_o, _sink_table(sinks, 8), bias)


def _swa_sample_kernel(x_ref, g_ref, wqkv_ref, bqkv_ref, wo_ref, bo_ref, sink_ref, bias_ref, ck_ref, cv_ref,
                       o_ref, nk_ref, nv_ref, z_ref, cat_ref, *, dec_len):
    ncol = SWA_HEADS // 2
    grp = ncol * dec_len
    x = x_ref[...]
    hn = _rms(x, g_ref[...]).astype(BF16)
    z_ref[...] = _dot(hn, wqkv_ref[...]) + bqkv_ref[...]

    bias, sink = bias_ref[...], sink_ref[...]
    lane_lo = lax.broadcasted_iota(jnp.int32, (grp, LANES), 1) < SWA_HD
    kv0 = lax.broadcasted_iota(jnp.int32, (grp, LANES), 0) < grp // 2
    own_half = jnp.where(kv0, 0, 1) == jnp.where(lane_lo, 0, 1)
    pad = jnp.zeros((WINDOW - dec_len, LANES), F32)
    elems = [slice(b * dec_len, (b + 1) * dec_len) for b in range(SAMPLE_BB)]
    values, scores = [], []
    for b, rows in enumerate(elems):
        knew = z_ref[rows, _SK0:_SK0 + LANES]
        vnew = z_ref[rows, _SV0:_SV0 + LANES]
        kold, vold = ck_ref[b], cv_ref[b]
        nk_ref[b, 0:WINDOW - dec_len, :] = kold[dec_len:]
        nk_ref[b, WINDOW - dec_len:, :] = knew
        nv_ref[b, 0:WINDOW - dec_len, :] = vold[dec_len:]
        nv_ref[b, WINDOW - dec_len:, :] = vnew
        kall = jnp.concatenate([kold, knew, pad], axis=0).astype(BF16)
        values.append(jnp.concatenate([vold, vnew, pad], axis=0).astype(BF16))

        q = jnp.concatenate([z_ref[rows, j * LANES:(j + 1) * LANES] for j in range(ncol)], axis=0)
        q = q * (SWA_HD ** -0.5 * _LOG2E)
        q_swapped = pltpu.roll(q, SWA_HD, 1)
        q_even = jnp.where(own_half, jnp.where(kv0, q, q_swapped), 0.0)
        q_odd = jnp.where(own_half, jnp.where(kv0, q_swapped, q), 0.0)
        lhs = jnp.concatenate([q_even, q_odd], axis=0).astype(BF16)
        scores.append(_dot_nt(lhs, kall) + bias)

    probs, dens = [], []
    for s in scores:
        m = jnp.max(jnp.maximum(jnp.maximum(s[:, :LANES], s[:, LANES:]), sink), axis=-1, keepdims=True)
        p = jnp.exp2(s - m)
        dens.append(jnp.sum(p, axis=-1, keepdims=True) + jnp.exp2(sink - m))
        probs.append(p.astype(BF16))

    outs = [_dot(p, v) for p, v in zip(probs, values)]

    for rows, out, den in zip(elems, outs, dens):
        o = out / den
        o_swapped = pltpu.roll(o, SWA_HD, 1)
        cols = jnp.where(lane_lo,
                         jnp.where(kv0, o[:grp], o_swapped[:grp]),
                         jnp.where(kv0, o_swapped[grp:], o[grp:]))
        for j in range(ncol):
            cat_ref[rows, j * LANES:(j + 1) * LANES] = cols[j * dec_len:(j + 1) * dec_len]

    o_ref[...] = x + _dot(cat_ref[...].astype(BF16), wo_ref[...]) + bo_ref[...]


def _swa_sample(x, g, w_qkv, b_qkv, sinks, w_o, b_o, cache_k, cache_v, dec_batch, dec_len):
    rows_n = SAMPLE_BB * dec_len
    ncol = SWA_HEADS // 2
    score_rows = 2 * ncol * dec_len
    sink = jnp.broadcast_to((sinks.astype(F32) * _LOG2E).reshape(ncol, 2).T[:, :, None, None],
                            (2, ncol, dec_len, LANES)).reshape(score_rows, LANES)
    bias = jnp.asarray(np.tile(_band_bias(dec_len, WINDOW)[:, :2 * WINDOW], (2 * ncol, 1)))
    row_spec = pl.BlockSpec((rows_n, D_MODEL), lambda i: (i, 0))
    cache_spec = pl.BlockSpec((SAMPLE_BB, WINDOW, LANES), lambda i: (i, 0, 0))
    return pl.pallas_call(
        functools.partial(_swa_sample_kernel, dec_len=dec_len),
        grid=(dec_batch // SAMPLE_BB,),
        in_specs=[
            row_spec,
            _const_spec((1, D_MODEL)),
            _const_spec((D_MODEL, SWA_IN)),
            _const_spec((1, SWA_IN)),
            _const_spec((SWA_OUT, D_MODEL)),
            _const_spec((1, D_MODEL)),
            _const_spec((score_rows, LANES)),
            _const_spec((score_rows, 2 * WINDOW)),
            cache_spec, cache_spec,
        ],
        out_specs=[row_spec, cache_spec, cache_spec],
        out_shape=[
            jax.ShapeDtypeStruct((dec_batch * dec_len, D_MODEL), F32),
            jax.ShapeDtypeStruct((dec_batch, WINDOW, LANES), F32),
            jax.ShapeDtypeStruct((dec_batch, WINDOW, LANES), F32),
        ],
        scratch_shapes=[
            pltpu.VMEM((rows_n, SWA_IN), F32),
            pltpu.VMEM((rows_n, SWA_OUT), F32),
        ],
        compiler_params=_params("arbitrary"),
        name="swa_sample",
    )(x, g, w_qkv, b_qkv, w_o, b_o, sink, bias, cache_k, cache_v)


def kernel(x_prompt, x_sample, state_ret, cache_swa_k, cache_swa_v, norm_mix, norm_mlp, norm_final, ab_w_in, ab_w_s, ab_b_s, ab_ln_g, ab_ln_b, ab_w_o, swa_w_qkv, swa_b_qkv, swa_sinks, swa_w_o, swa_b_o, mlp_w_up, mlp_w_down):
    batch, seq, _ = x_prompt.shape
    dec_batch, dec_len, _ = x_sample.shape
    assert seq % MIXER_ROWS == 0 and seq % RET_CHUNK == 0
    kv_lanes = SWA_KV_HEADS * SWA_HD

    hp = x_prompt.reshape(batch * seq, D_MODEL)
    hs = x_sample.reshape(dec_batch * dec_len, D_MODEL)
    row = lambda v: v.reshape(1, -1)
    gfin = row(norm_final)

    w_in, w_o = ab_w_in[0].astype(BF16), ab_w_o[0].astype(BF16)
    hp, ret_p, (w_up, w_dn) = _ab_prompt(hp, row(norm_mix[0]), w_in, w_o, ab_w_s[0], ab_b_s[0], ab_ln_g[0],
                                         ab_ln_b[0], batch, seq, cast=[(mlp_w_up, 0), (mlp_w_down, 0)])
    hs, ret_s, gm_s = _ab_sample(hs, row(norm_mix[0]), w_in, w_o, ab_w_s[0], ab_b_s[0], ab_ln_g[0], ab_ln_b[0],
                                 state_ret[0], dec_batch, dec_len)
    hp, (w_qkv, w_so, w_up1, w_dn1) = _mlp(hp, row(norm_mlp[0]), w_up, w_dn, gfin, final_norm=False,
                                           name="mlp0_prompt",
                                           cast=[(swa_w_qkv, 0), (swa_w_o, 0), (mlp_w_up, 1), (mlp_w_down, 1)])
    hs, _ = _mlp(hs, row(norm_mlp[0]), w_up, w_dn, gfin, final_norm=False, name="mlp0_sample")

    hp, kp, vp = _swa_prompt(hp, row(norm_mix[1]), w_qkv, row(swa_b_qkv[0]), swa_sinks[0], w_so, row(swa_b_o[0]),
                             batch, seq)
    hs, ks, vs = _swa_sample(hs, row(norm_mix[1]), w_qkv, row(swa_b_qkv[0]), swa_sinks[0], w_so, row(swa_b_o[0]),
                             cache_swa_k[0].reshape(dec_batch, WINDOW, kv_lanes),
                             cache_swa_v[0].reshape(dec_batch, WINDOW, kv_lanes), dec_batch, dec_len)
    hp, _ = _mlp(hp, row(norm_mlp[1]), w_up1, w_dn1, gfin, final_norm=True, name="mlp1_prompt")
    hs, _ = _mlp(hs, row(norm_mlp[1]), w_up1, w_dn1, gfin, final_norm=True, name="mlp1_sample")

    cache_shape = lambda n: (1, n, WINDOW, SWA_KV_HEADS, SWA_HD)
    return (hp.reshape(batch, seq, D_MODEL),
            hs.reshape(dec_batch, dec_len, D_MODEL),
            ret_p[None],
            ret_s[None],
            gm_s.reshape(1, dec_batch, dec_len, GM_GROUPS * GM_GROUP_DIM),
            kp.reshape(cache_shape(batch)),
            vp.reshape(cache_shape(batch)),
            ks.reshape(cache_shape(dec_batch)),
            vs.reshape(cache_shape(dec_batch)))
```

```python
import functools

import numpy as np
import jax
import jax.numpy as jnp
from jax import lax
from jax.experimental import pallas as pl
from jax.experimental.pallas import tpu as pltpu

F32 = jnp.float32
BF16 = jnp.bfloat16

D_MODEL = 1024
PAST_LEN = 16384
RMS_EPS = 1e-6
LN_EPS = 1e-5
RET_HEADS = 4
RET_DK = 128
RET_DV = 128
RET_CHUNK = 128
ROPE_BASE = 10000.0
GM_GROUPS = 4
GM_GROUP_DIM = 128
GM_CHUNK = 128
AB_IN = 3072
AB_OUT = 1024
SWA_HEADS = 16
SWA_KV_HEADS = 2
SWA_HD = 64
WINDOW = 128
SWA_IN = 1280
SWA_OUT = 1024
D_FF = 4096

_Q0, _K0, _V0, _G0, _U0, _GV0 = 0, 512, 1024, 1536, 2048, 2560
_SK0, _SV0 = 1024, 1152

LANES = 128
MIXER_ROWS = 1024
MIXER_SPLIT = 4
MLP_ROWS = 1024
MLP_SPLIT = 4
FF_CHUNK = 1024
SAMPLE_BB = 16
VMEM_LIMIT = 56 * 1024 * 1024


def _rms(x, g):
    return x * lax.rsqrt(jnp.mean(x * x, axis=-1, keepdims=True) + RMS_EPS) * g


def _gelu(x):
    return 0.5 * x * (1.0 + lax.erf(x * np.float32(np.sqrt(0.5))))


def _silu(x):
    return x / (1.0 + jnp.exp(-x))


def _dot(a, b):
    return jnp.dot(a, b, preferred_element_type=F32)


def _dot_nt(a, b):
    return lax.dot_general(a, b, (((1,), (1,)), ((), ())), preferred_element_type=F32)


def _dot_tn(a, b):
    return lax.dot_general(a, b, (((0,), (0,)), ((), ())), preferred_element_type=F32)


def _const_spec(shape):
    zeros = (0,) * len(shape)
    return pl.BlockSpec(shape, lambda *_: zeros, pipeline_mode=pl.Buffered(1))


def _params(*sem):
    return pltpu.CompilerParams(dimension_semantics=sem, vmem_limit_bytes=VMEM_LIMIT)


def _cast_specs(weights, steps, flat_step):
    def src_map(layer):
        return lambda *g: (layer, flat_step(*g), 0)

    in_specs, out_specs, out_shapes = [], [], []
    for w, layer in weights:
        _, rows, cols = w.shape
        slab = rows // steps
        assert slab * steps == rows and slab % 16 == 0, (w.shape, steps)
        in_specs.append(pl.BlockSpec((None, slab, cols), src_map(layer)))
        out_specs.append(pl.BlockSpec((slab, cols), lambda *g: (flat_step(*g), 0)))
        out_shapes.append(jax.ShapeDtypeStruct((rows, cols), BF16))
    return in_specs, out_specs, out_shapes


def _cast_slabs(src_refs, dst_refs):
    for src, dst in zip(src_refs, dst_refs):
        dst[...] = src[...].astype(BF16)


def _mlp_kernel(x_ref, g_ref, wup_ref, wdn_ref, gf_ref, *rest, final_norm, n_cast):
    cast_src, o_ref, cast_dst = rest[:n_cast], rest[n_cast], rest[n_cast + 1:]
    _cast_slabs(cast_src, cast_dst)
    rows = x_ref.shape[0] // MLP_SPLIT
    groups = [slice(i * rows, (i + 1) * rows) for i in range(MLP_SPLIT)]
    xs = [x_ref[r, :] for r in groups]
    hns = [_rms(x, g_ref[...]).astype(BF16) for x in xs]
    accs = [jnp.zeros_like(x) for x in xs]
    for c in range(D_FF // FF_CHUNK):
        cols = slice(c * FF_CHUNK, (c + 1) * FF_CHUNK)
        for i in range(MLP_SPLIT):
            h = _dot(hns[i], wup_ref[:, cols])
            h = jnp.square(jnp.maximum(h, 0.0)).astype(BF16)
            accs[i] = accs[i] + _dot(h, wdn_ref[cols, :])
    for i, r in enumerate(groups):
        y = xs[i] + accs[i]
        if final_norm:
            y = _rms(y, gf_ref[...])
        o_ref[r, :] = y


def _mlp(x, g, wup, wdn, gf, *, final_norm, name, cast=()):
    rows = x.shape[0]
    tm = min(MLP_ROWS, rows)
    steps = rows // tm
    row_spec = pl.BlockSpec((tm, D_MODEL), lambda i: (i, 0))
    cast_in, cast_out, cast_shapes = _cast_specs(cast, steps, lambda i: i)
    out, *cast_w = pl.pallas_call(
        functools.partial(_mlp_kernel, final_norm=final_norm, n_cast=len(cast)),
        grid=(steps,),
        in_specs=[
            row_spec,
            _const_spec((1, D_MODEL)),
            _const_spec((D_MODEL, D_FF)),
            _const_spec((D_FF, D_MODEL)),
            _const_spec((1, D_MODEL)),
            *cast_in,
        ],
        out_specs=[row_spec, *cast_out],
        out_shape=[jax.ShapeDtypeStruct((rows, D_MODEL), F32), *cast_shapes],
        compiler_params=_params("arbitrary"),
        name=name,
    )(x, g, wup, wdn, gf, *(w for w, _ in cast))
    return out, cast_w


def _rotary(x, cos2, sin2):
    return x * cos2 + pltpu.roll(x, 64, 1) * sin2


def _retention_tables(chunk):
    log_g = np.log1p(-np.exp2(-5.0 - np.arange(RET_HEADS, dtype=np.float64)))
    idx = np.arange(chunk, dtype=np.float64)
    diff = idx[:, None] - idx[None, :]
    decay = np.where(diff[None] >= 0, np.exp(log_g[:, None, None] * np.maximum(diff, 0.0)[None]), 0.0)
    q_dec = np.exp(log_g[:, None] * (idx[None, :] + 1.0))
    k_dec = np.exp(log_g[:, None] * (chunk - 1.0 - idx[None, :]))
    chunk_dec = tuple(float(v) for v in np.exp(log_g * chunk))
    f32 = lambda t: t.astype(np.float32)
    return f32(decay), f32(q_dec), f32(k_dec), chunk_dec


def _rotary_tables(pos):
    half = RET_DK // 2
    inv = ROPE_BASE ** (-np.arange(half, dtype=np.float64) / half)
    ang = np.asarray(pos, np.float64)[:, None] * inv[None, :]
    cos, sin = np.cos(ang), np.sin(ang)
    cos2 = np.concatenate([cos, cos], axis=-1)
    sin2 = np.concatenate([-sin, sin], axis=-1)
    scale = RET_DK ** -0.5
    return np.concatenate([cos2 * scale, sin2 * scale, cos2, sin2], axis=-1).astype(np.float32)


def _gmlp_branch(z_ref, rows, lng_ref, lnb_ref, gi):
    u = _gelu(z_ref[rows, _U0 + gi * LANES:_U0 + (gi + 1) * LANES])
    gv = _gelu(z_ref[rows, _GV0 + gi * LANES:_GV0 + (gi + 1) * LANES])
    mu = jnp.mean(gv, axis=-1, keepdims=True)
    cen = gv - mu
    var = jnp.mean(cen * cen, axis=-1, keepdims=True)
    gvn = cen * lax.rsqrt(var + LN_EPS) * lng_ref[gi:gi + 1, :] + lnb_ref[gi:gi + 1, :]
    return u, gvn


def _ab_prompt_kernel(x_ref, g_ref, win_ref, wo_ref, rot_ref,
                      dec_ref, qd_ref, kd_ref, ws_ref, bs_ref, lng_ref, lnb_ref,
                      *rest, chunk_dec, n_cast):
    cast_src, (o_ref, st_ref), cast_dst = rest[:n_cast], rest[n_cast:n_cast + 2], rest[n_cast + 2:2 * n_cast + 2]
    scratch = rest[2 * n_cast + 2:]
    _cast_slabs(cast_src, cast_dst)
    si = pl.program_id(1)
    z_refs, cat_refs, s_ref = scratch[:MIXER_SPLIT], scratch[MIXER_SPLIT:2 * MIXER_SPLIT], scratch[-1]
    sub_rows = MIXER_ROWS // MIXER_SPLIT
    nchunk = sub_rows // RET_CHUNK

    @pl.when(si == 0)
    def _():
        s_ref[...] = jnp.zeros_like(s_ref)

    ri = lax.broadcasted_iota(jnp.int32, (GM_CHUNK, GM_CHUNK), 0)
    ci = lax.broadcasted_iota(jnp.int32, (GM_CHUNK, GM_CHUNK), 1)
    w_spatial = [jnp.where(ci <= ri, ws_ref[gi], 0.0).astype(BF16) for gi in range(GM_GROUPS)]
    tile_rows = lambda t: jnp.concatenate([t] * nchunk, axis=0)
    states = [s_ref[h] for h in range(RET_HEADS)]

    for sub in range(MIXER_SPLIT):
        rows = slice(sub * sub_rows, (sub + 1) * sub_rows)
        chunks = [slice(c * RET_CHUNK, (c + 1) * RET_CHUNK) for c in range(nchunk)]
        z_ref, cat_ref = z_refs[sub], cat_refs[sub]
        x = x_ref[rows, :]
        hn = _rms(x, g_ref[...]).astype(BF16)
        z_ref[:, :_U0] = _dot(hn, win_ref[:, :_U0])
        z_ref[:, _U0:] = _dot(hn, win_ref[:, _U0:])

        cq, sq, ck, sk = (rot_ref[rows, i * LANES:(i + 1) * LANES] for i in range(4))
        stage1 = []
        for h in range(RET_HEADS):
            q = _rotary(z_ref[:, _Q0 + h * LANES:_Q0 + (h + 1) * LANES], cq, sq)
            k = _rotary(z_ref[:, _K0 + h * LANES:_K0 + (h + 1) * LANES], ck, sk)
            v = z_ref[:, _V0 + h * LANES:_V0 + (h + 1) * LANES].astype(BF16)
            qb = q.astype(BF16)
            qdb = (q * tile_rows(qd_ref[h])).astype(BF16)
            local = [slice(c * RET_CHUNK, (c + 1) * RET_CHUNK) for c in range(nchunk)]
            kts = [k[r].T for r in local]
            scores = [_dot(qb[r], kt.astype(BF16)) for r, kt in zip(local, kts)]
            kvs = [_dot((kt * kd_ref[h]).astype(BF16), v[r]) for r, kt in zip(local, kts)]
            before = []
            for kv in kvs:
                before.append(states[h].astype(BF16))
                states[h] = states[h] * chunk_dec[h] + kv
            stage1.append((v, qdb, scores, before, local))
        for h, (v, qdb, scores, before, local) in enumerate(stage1):
            lanes = slice(h * LANES, (h + 1) * LANES)
            dec = dec_ref[h]
            for c, r in enumerate(local):
                lhs = jnp.concatenate([(scores[c] * dec).astype(BF16), qdb[r]], axis=1)
                rhs = jnp.concatenate([v[r], before[c]], axis=0)
                o = _dot(lhs, rhs)
                o = o * lax.rsqrt(jnp.mean(o * o, axis=-1, keepdims=True) + RMS_EPS)
                gate = _silu(z_ref[chunks[c], _G0 + h * LANES:_G0 + (h + 1) * LANES])
                cat_ref[chunks[c], lanes] = (gate * o).astype(BF16)

        for gi in range(GM_GROUPS):
            u, gvn = _gmlp_branch(z_ref, slice(None), lng_ref, lnb_ref, gi)
            gvb = gvn.astype(BF16)
            rhs = jnp.concatenate([gvb[c * GM_CHUNK:(c + 1) * GM_CHUNK] for c in range(nchunk)], axis=1)
            mixed = _dot(w_spatial[gi], rhs)
            bias = bs_ref[gi]
            for c in range(nchunk):
                local_rows = slice(c * GM_CHUNK, (c + 1) * GM_CHUNK)
                m = mixed[:, c * GM_CHUNK:(c + 1) * GM_CHUNK] + bias
                cat_ref[chunks[c], 512 + gi * LANES:512 + (gi + 1) * LANES] = (u[local_rows] * m).astype(BF16)

        o_ref[rows, :] = x + _dot(cat_ref[...], wo_ref[...])

    for h in range(RET_HEADS):
        s_ref[h] = states[h]

    @pl.when(si == pl.num_programs(1) - 1)
    def _():
        st_ref[...] = s_ref[...]


def _ab_prompt(x, g, w_in, w_o, w_s, b_s, ln_g, ln_b, batch, seq, cast=()):
    nstep = seq // MIXER_ROWS
    cast_in, cast_out, cast_shapes = _cast_specs(cast, batch * nstep, lambda b, s: b * nstep + s)
    decay, q_dec, k_dec, chunk_dec = _retention_tables(RET_CHUNK)
    rot = _rotary_tables(np.arange(seq))
    lane_b = lambda t: (np if isinstance(t, np.ndarray) else jnp).broadcast_to(t[:, :, None], t.shape + (LANES,))
    row_spec = pl.BlockSpec((MIXER_ROWS, D_MODEL), lambda b, s: (b * nstep + s, 0))
    rot_spec = pl.BlockSpec((MIXER_ROWS, 4 * LANES), lambda b, s: (s, 0))
    cube = (RET_HEADS, RET_CHUNK, LANES)
    out, state, *cast_w = pl.pallas_call(
        functools.partial(_ab_prompt_kernel, chunk_dec=chunk_dec, n_cast=len(cast)),
        grid=(batch, nstep),
        in_specs=[
            row_spec,
            _const_spec((1, D_MODEL)),
            _const_spec((D_MODEL, AB_IN)),
            _const_spec((AB_OUT, D_MODEL)),
            rot_spec,
            _const_spec(cube), _const_spec(cube), _const_spec(cube),
            _const_spec(cube), _const_spec(cube),
            _const_spec((GM_GROUPS, GM_GROUP_DIM)), _const_spec((GM_GROUPS, GM_GROUP_DIM)),
            *cast_in,
        ],
        out_specs=[
            row_spec,
            pl.BlockSpec((None, RET_HEADS, RET_DK, RET_DV), lambda b, s: (b, 0, 0, 0)),
            *cast_out,
        ],
        out_shape=[
            jax.ShapeDtypeStruct((batch * seq, D_MODEL), F32),
            jax.ShapeDtypeStruct((batch, RET_HEADS, RET_DK, RET_DV), F32),
            *cast_shapes,
        ],
        scratch_shapes=(
            [pltpu.VMEM((MIXER_ROWS // MIXER_SPLIT, AB_IN), F32)] * MIXER_SPLIT
            + [pltpu.VMEM((MIXER_ROWS // MIXER_SPLIT, AB_OUT), BF16)] * MIXER_SPLIT
            + [pltpu.VMEM((RET_HEADS, RET_DK, RET_DV), F32)]),
        compiler_params=_params("arbitrary", "arbitrary"),
        name="ab_prompt",
    )(x, g, w_in, w_o, rot, decay, lane_b(q_dec), np.broadcast_to(k_dec[:, None, :], cube),
      w_s, lane_b(b_s), ln_g, ln_b, *(w for w, _ in cast))
    return out, state, cast_w


def _ab_sample_kernel(x_ref, g_ref, win_ref, wo_ref, rot_ref,
                      dec_ref, qd_ref, kd_ref, ws_ref, bs_ref, lng_ref, lnb_ref, causal_ref, st_ref,
                      o_ref, sto_ref, gvn_ref, z_ref, cat_ref, *, chunk_dec, dec_len):
    rows_n = SAMPLE_BB * dec_len
    x = x_ref[...]
    hn = _rms(x, g_ref[...]).astype(BF16)
    z_ref[...] = _dot(hn, win_ref[...])

    cq, sq, ck, sk = (rot_ref[:, i * LANES:(i + 1) * LANES] for i in range(4))
    token = lax.broadcasted_iota(jnp.int32, (rows_n, rows_n), 1)
    for h in range(RET_HEADS):
        lanes = slice(h * LANES, (h + 1) * LANES)
        q = _rotary(z_ref[:, _Q0 + h * LANES:_Q0 + (h + 1) * LANES], cq, sq)
        k = _rotary(z_ref[:, _K0 + h * LANES:_K0 + (h + 1) * LANES], ck, sk)
        v = z_ref[:, _V0 + h * LANES:_V0 + (h + 1) * LANES].astype(BF16)
        scores = _dot_nt(q.astype(BF16), k.astype(BF16)) * dec_ref[h]
        intra = _dot(scores.astype(BF16), v)
        qs = q * qd_ref[h]
        kst = (k * kd_ref[h]).T
        kst_all = jnp.concatenate(
            [jnp.where((token >= b * dec_len) & (token < (b + 1) * dec_len), kst, 0.0).astype(BF16)
             for b in range(SAMPLE_BB)], axis=0)
        kv_all = _dot(kst_all, v)
        cross = []
        for b in range(SAMPLE_BB):
            rows = slice(b * dec_len, (b + 1) * dec_len)
            state = st_ref[b, h]
            cross.append(_dot(qs[rows].astype(BF16), state.astype(BF16)))
            sto_ref[b, h] = state * chunk_dec[h] + kv_all[b * RET_DK:(b + 1) * RET_DK]
        o = intra + jnp.concatenate(cross, axis=0)
        o = o * lax.rsqrt(jnp.mean(o * o, axis=-1, keepdims=True) + RMS_EPS)
        gate = _silu(z_ref[:, _G0 + h * LANES:_G0 + (h + 1) * LANES])
        cat_ref[:, lanes] = (gate * o).astype(BF16)

    causal = causal_ref[...] > 0.0
    for gi in range(GM_GROUPS):
        u, gvn = _gmlp_branch(z_ref, slice(None), lng_ref, lnb_ref, gi)
        gvn_ref[:, gi * LANES:(gi + 1) * LANES] = gvn
        w = jnp.where(causal, ws_ref[gi], 0.0).astype(BF16)
        mixed = _dot(w, gvn.astype(BF16)) + bs_ref[gi]
        cat_ref[:, 512 + gi * LANES:512 + (gi + 1) * LANES] = (u * mixed).astype(BF16)

    o_ref[...] = x + _dot(cat_ref[...], wo_ref[...])


def _ab_sample(x, g, w_in, w_o, w_s, b_s, ln_g, ln_b, state, dec_batch, dec_len):
    rows_n = SAMPLE_BB * dec_len
    assert rows_n == LANES and dec_batch % SAMPLE_BB == 0
    decay, q_dec, k_dec, chunk_dec = _retention_tables(dec_len)
    rot = np.tile(_rotary_tables(PAST_LEN + np.arange(dec_len)), (SAMPLE_BB, 1))
    bd_decay = np.stack([np.kron(np.eye(SAMPLE_BB, dtype=np.float32), d) for d in decay])

    def rows_b(t):
        xp = np if isinstance(t, np.ndarray) else jnp
        return xp.broadcast_to(xp.tile(t, (1, SAMPLE_BB))[:, :, None], (t.shape[0], rows_n, LANES))
    w_tiled = jnp.tile(w_s[:, :dec_len, :dec_len], (1, SAMPLE_BB, SAMPLE_BB))
    causal = jnp.asarray(np.kron(np.eye(SAMPLE_BB), np.tril(np.ones((dec_len, dec_len)))), F32)
    row_spec = pl.BlockSpec((rows_n, D_MODEL), lambda i: (i, 0))
    st_spec = pl.BlockSpec((SAMPLE_BB, RET_HEADS, RET_DK, RET_DV), lambda i: (i, 0, 0, 0))
    cube = (RET_HEADS, rows_n, LANES)
    tab = (rows_n, LANES)
    return pl.pallas_call(
        functools.partial(_ab_sample_kernel, chunk_dec=chunk_dec, dec_len=dec_len),
        grid=(dec_batch // SAMPLE_BB,),
        in_specs=[
            row_spec,
            _const_spec((1, D_MODEL)),
            _const_spec((D_MODEL, AB_IN)),
            _const_spec((AB_OUT, D_MODEL)),
            _const_spec((rows_n, 4 * LANES)),
            _const_spec(cube), _const_spec(cube), _const_spec(cube),
            _const_spec(cube), _const_spec(cube),
            _const_spec((GM_GROUPS, GM_GROUP_DIM)), _const_spec((GM_GROUPS, GM_GROUP_DIM)),
            _const_spec(tab),
            st_spec,
        ],
        out_specs=[
            row_spec,
            st_spec,
            pl.BlockSpec((rows_n, GM_GROUPS * GM_GROUP_DIM), lambda i: (i, 0)),
        ],
        out_shape=[
            jax.ShapeDtypeStruct((dec_batch * dec_len, D_MODEL), F32),
            jax.ShapeDtypeStruct(state.shape, F32),
            jax.ShapeDtypeStruct((dec_batch * dec_len, GM_GROUPS * GM_GROUP_DIM), F32),
        ],
        scratch_shapes=[
            pltpu.VMEM((rows_n, AB_IN), F32),
            pltpu.VMEM((rows_n, AB_OUT), BF16),
        ],
        compiler_params=_params("arbitrary"),
        name="ab_sample",
    )(x, g, w_in, w_o, rot, bd_decay, rows_b(q_dec), rows_b(k_dec),
      w_tiled, rows_b(b_s[:, :dec_len]), ln_g, ln_b, causal, state)


_LOG2E = float(np.log2(np.e))


def _softmax_numerators(s, sink_even, sink_odd):
    s0, s1, s2, s3 = (s[:, i * LANES:(i + 1) * LANES] for i in range(4))
    m_even = jnp.max(jnp.maximum(jnp.maximum(s0, s1), sink_even), axis=-1, keepdims=True)
    m_odd = jnp.max(jnp.maximum(jnp.maximum(s2, s3), sink_odd), axis=-1, keepdims=True)
    p = jnp.concatenate([jnp.exp2(s0 - m_even), jnp.exp2(s1 - m_even),
                         jnp.exp2(s2 - m_odd), jnp.exp2(s3 - m_odd)], axis=1)
    even_lane = lax.broadcasted_iota(jnp.int32, sink_even.shape, 1) < SWA_HD
    sink_term = jnp.where(even_lane, jnp.exp2(sink_even - m_even), jnp.exp2(sink_odd - m_odd))
    return p.astype(BF16), sink_term


def _band_bias(q_rows, first_key_row):
    i = np.arange(q_rows)[:, None]
    j = np.arange(2 * WINDOW)[None, :]
    rel = i + first_key_row - j
    half = np.where((rel >= 0) & (rel <= WINDOW), 0.0, -np.inf).astype(np.float32)
    return np.concatenate([half, half], axis=1)


def _kv_layouts(kcol, vcol):
    even_lane = lax.broadcasted_iota(jnp.int32, kcol.shape, 1) < SWA_HD
    ks = kcol * (SWA_HD ** -0.5 * _LOG2E)
    kr = pltpu.roll(ks, SWA_HD, 1)
    vr = pltpu.roll(vcol, SWA_HD, 1)
    one_even = jnp.where(even_lane, 1.0, 0.0)
    one_odd = 1.0 - one_even
    zero = jnp.zeros_like(kcol)
    out = []
    for kvh in range(SWA_KV_HEADS):
        k_lo, k_hi = (ks, kr) if kvh == 0 else (kr, ks)
        v_lo, v_hi = (vcol, vr) if kvh == 0 else (vr, vcol)
        ktop = jnp.where(even_lane, k_lo, zero).astype(BF16)
        kbot = jnp.where(even_lane, zero, k_hi).astype(BF16)
        vtop = jnp.concatenate([jnp.where(even_lane, v_lo, zero), one_even], axis=1).astype(BF16)
        vbot = jnp.concatenate([jnp.where(even_lane, zero, v_hi), one_odd], axis=1).astype(BF16)
        out.append((ktop, kbot, vtop, vbot))
    return out


def _swa_prompt_kernel(x_ref, xh_ref, g_ref, wqkv_ref, bqkv_ref, wo_ref, bo_ref, sink_ref, bias_ref,
                       o_ref, kc_ref, vc_ref, z_ref, cat_ref):
    si = pl.program_id(1)
    nblk = MIXER_ROWS // WINDOW
    npair = SWA_HEADS // SWA_KV_HEADS // 2
    sub_rows = MIXER_ROWS // MIXER_SPLIT

    x = x_ref[...]
    hn = _rms(x, g_ref[...]).astype(BF16)
    for sub in range(MIXER_SPLIT):
        rows = slice(sub * sub_rows, (sub + 1) * sub_rows)
        z_ref[rows, :] = _dot(hn[rows], wqkv_ref[...]) + bqkv_ref[...]
    hh = _rms(xh_ref[...], g_ref[...]).astype(BF16)
    zh = _dot(hh, wqkv_ref[:, _SK0:]) + bqkv_ref[:, _SK0:]

    kcol = z_ref[:, _SK0:_SK0 + LANES]
    vcol = z_ref[:, _SV0:_SV0 + LANES]
    kc_ref[...] = kcol[MIXER_ROWS - WINDOW:]
    vc_ref[...] = vcol[MIXER_ROWS - WINDOW:]
    layouts = _kv_layouts(jnp.concatenate([zh[:, :LANES], kcol], axis=0),
                          jnp.concatenate([zh[:, LANES:], vcol], axis=0))

    sink_rows = lambda tab: jnp.concatenate(
        [tab[p * 8:(p + 1) * 8] for p in range(npair) for _ in range(WINDOW // 8)], axis=0)
    sinks = [[sink_rows(sink_ref[kvh, par]) for par in range(2)] for kvh in range(SWA_KV_HEADS)]

    for r in range(nblk):
        rows = slice(r * WINDOW, (r + 1) * WINDOW)
        keys = slice(r * WINDOW, (r + 2) * WINDOW)
        tile = bias_ref[jnp.where(si > 0, 0, 1)] if r == 0 else bias_ref[0]
        bias = jnp.concatenate([tile] * npair, axis=0)
        for kvh, (ktop, kbot, vtop, vbot) in enumerate(layouts):
            c0 = kvh * npair * LANES
            lhs = jnp.concatenate(
                [z_ref[rows, c0 + p * LANES:c0 + (p + 1) * LANES] for p in range(npair)], axis=0).astype(BF16)
            s = _dot_nt(lhs, jnp.concatenate([ktop[keys], kbot[keys]], axis=0)) + bias
            p, sink_term = _softmax_numerators(s, sinks[kvh][0], sinks[kvh][1])
            out = _dot(p, jnp.concatenate([vtop[keys], vbot[keys]], axis=0))
            o = out[:, :LANES] / (out[:, LANES:] + sink_term)
            for p_i in range(npair):
                cat_ref[rows, c0 + p_i * LANES:c0 + (p_i + 1) * LANES] = (
                    o[p_i * WINDOW:(p_i + 1) * WINDOW].astype(BF16))

    o_ref[...] = x + _dot(cat_ref[...], wo_ref[...]) + bo_ref[...]


def _sink_table(sinks, rows_per_pair):
    npair = SWA_HEADS // SWA_KV_HEADS // 2
    s = (sinks.astype(F32) * _LOG2E).reshape(SWA_KV_HEADS, npair, 2).transpose(0, 2, 1)[:, :, :, None, None]
    s = jnp.broadcast_to(s, (SWA_KV_HEADS, 2, npair, rows_per_pair, LANES))
    return s.reshape(SWA_KV_HEADS, 2, npair * rows_per_pair, LANES)


def _swa_prompt(x, g, w_qkv, b_qkv, sinks, w_o, b_o, batch, seq):
    nstep = seq // MIXER_ROWS
    npair = SWA_HEADS // SWA_KV_HEADS // 2
    row_spec = pl.BlockSpec((MIXER_ROWS, D_MODEL), lambda b, s: (b * nstep + s, 0))
    nblk = MIXER_ROWS // WINDOW
    halo_spec = pl.BlockSpec((WINDOW, D_MODEL), lambda b, s: (jnp.maximum((b * nstep + s) * nblk - 1, 0), 0))
    cache_spec = pl.BlockSpec((None, WINDOW, LANES), lambda b, s: (b, 0, 0))
    band = _band_bias(WINDOW, WINDOW)
    no_prev = np.where(np.arange(4 * WINDOW)[None, :] % (2 * WINDOW) >= WINDOW, band, -np.inf).astype(np.float32)
    bias = jnp.asarray(np.stack([band, no_prev]))
    return pl.pallas_call(
        _swa_prompt_kernel,
        grid=(batch, nstep),
        in_specs=[
            row_spec,
            halo_spec,
            _const_spec((1, D_MODEL)),
            _const_spec((D_MODEL, SWA_IN)),
            _const_spec((1, SWA_IN)),
            _const_spec((SWA_OUT, D_MODEL)),
            _const_spec((1, D_MODEL)),
            _const_spec((SWA_KV_HEADS, 2, npair * 8, LANES)),
            _const_spec((2, WINDOW, 4 * WINDOW)),
        ],
        out_specs=[row_spec, cache_spec, cache_spec],
        out_shape=[
            jax.ShapeDtypeStruct((batch * seq, D_MODEL), F32),
            jax.ShapeDtypeStruct((batch, WINDOW, LANES), F32),
            jax.ShapeDtypeStruct((batch, WINDOW, LANES), F32),
        ],
        scratch_shapes=[
            pltpu.VMEM((MIXER_ROWS, SWA_IN), F32),
            pltpu.VMEM((MIXER_ROWS, SWA_OUT), BF16),
        ],
        compiler_params=_params("arbitrary", "arbitrary"),
        name="swa_prompt",
    )(x, x, g, w_qkv, b_qkv, w_o, b_o, _sink_table(sinks, 8), bias)


def _swa_sample_kernel(x_ref, g_ref, wqkv_ref, bqkv_ref, wo_ref, bo_ref, sink_ref, bias_ref, ck_ref, cv_ref,
                       o_ref, nk_ref, nv_ref, z_ref, cat_ref, *, dec_len):
    ncol = SWA_HEADS // 2
    grp = ncol * dec_len
    x = x_ref[...]
    hn = _rms(x, g_ref[...]).astype(BF16)
    z_ref[...] = _dot(hn, wqkv_ref[...]) + bqkv_ref[...]

    bias, sink = bias_ref[...], sink_ref[...]
    lane_lo = lax.broadcasted_iota(jnp.int32, (grp, LANES), 1) < SWA_HD
    kv0 = lax.broadcasted_iota(jnp.int32, (grp, LANES), 0) < grp // 2
    own_half = jnp.where(kv0, 0, 1) == jnp.where(lane_lo, 0, 1)
    pad = jnp.zeros((WINDOW - dec_len, LANES), F32)
    elems = [slice(b * dec_len, (b + 1) * dec_len) for b in range(SAMPLE_BB)]
    values, scores = [], []
    for b, rows in enumerate(elems):
        knew = z_ref[rows, _SK0:_SK0 + LANES]
        vnew = z_ref[rows, _SV0:_SV0 + LANES]
        kold, vold = ck_ref[b].T, cv_ref[b].T
        nk_ref[b, 0:WINDOW - dec_len, :] = kold[dec_len:]
        nk_ref[b, WINDOW - dec_len:, :] = knew
        nv_ref[b, 0:WINDOW - dec_len, :] = vold[dec_len:]
        nv_ref[b, WINDOW - dec_len:, :] = vnew
        kall = jnp.concatenate([kold, knew, pad], axis=0).astype(BF16)
        values.append(jnp.concatenate([vold, vnew, pad], axis=0).astype(BF16))

        q = jnp.concatenate([z_ref[rows, j * LANES:(j + 1) * LANES] for j in range(ncol)], axis=0)
        q = q * (SWA_HD ** -0.5 * _LOG2E)
        q_swapped = pltpu.roll(q, SWA_HD, 1)
        q_even = jnp.where(own_half, jnp.where(kv0, q, q_swapped), 0.0)
        q_odd = jnp.where(own_half, jnp.where(kv0, q_swapped, q), 0.0)
        lhs = jnp.concatenate([q_even, q_odd], axis=0).astype(BF16)
        scores.append(_dot_nt(lhs, kall) + bias)

    probs, dens = [], []
    for s in scores:
        m = jnp.max(jnp.maximum(jnp.maximum(s[:, :LANES], s[:, LANES:]), sink), axis=-1, keepdims=True)
        p = jnp.exp2(s - m)
        dens.append(jnp.sum(p, axis=-1, keepdims=True) + jnp.exp2(sink - m))
        probs.append(p.astype(BF16))

    outs = [_dot(p, v) for p, v in zip(probs, values)]

    for rows, out, den in zip(elems, outs, dens):
        o = out / den
        o_swapped = pltpu.roll(o, SWA_HD, 1)
        cols = jnp.where(lane_lo,
                         jnp.where(kv0, o[:grp], o_swapped[:grp]),
                         jnp.where(kv0, o_swapped[grp:], o[grp:]))
        for j in range(ncol):
            cat_ref[rows, j * LANES:(j + 1) * LANES] = cols[j * dec_len:(j + 1) * dec_len]

    o_ref[...] = x + _dot(cat_ref[...].astype(BF16), wo_ref[...]) + bo_ref[...]


def _swa_sample(x, g, w_qkv, b_qkv, sinks, w_o, b_o, cache_k, cache_v, dec_batch, dec_len):
    rows_n = SAMPLE_BB * dec_len
    ncol = SWA_HEADS // 2
    score_rows = 2 * ncol * dec_len
    sink = jnp.broadcast_to((sinks.astype(F32) * _LOG2E).reshape(ncol, 2).T[:, :, None, None],
                            (2, ncol, dec_len, LANES)).reshape(score_rows, LANES)
    bias = jnp.asarray(np.tile(_band_bias(dec_len, WINDOW)[:, :2 * WINDOW], (2 * ncol, 1)))
    row_spec = pl.BlockSpec((rows_n, D_MODEL), lambda i: (i, 0))
    cache_spec = pl.BlockSpec((SAMPLE_BB, WINDOW, LANES), lambda i: (i, 0, 0))
    return pl.pallas_call(
        functools.partial(_swa_sample_kernel, dec_len=dec_len),
        grid=(dec_batch // SAMPLE_BB,),
        in_specs=[
            row_spec,
            _const_spec((1, D_MODEL)),
            _const_spec((D_MODEL, SWA_IN)),
            _const_spec((1, SWA_IN)),
            _const_spec((SWA_OUT, D_MODEL)),
            _const_spec((1, D_MODEL)),
            _const_spec((score_rows, LANES)),
            _const_spec((score_rows, 2 * WINDOW)),
            cache_spec, cache_spec,
        ],
        out_specs=[row_spec, cache_spec, cache_spec],
        out_shape=[
            jax.ShapeDtypeStruct((dec_batch * dec_len, D_MODEL), F32),
            jax.ShapeDtypeStruct((dec_batch, WINDOW, LANES), F32),
            jax.ShapeDtypeStruct((dec_batch, WINDOW, LANES), F32),
        ],
        scratch_shapes=[
            pltpu.VMEM((rows_n, SWA_IN), F32),
            pltpu.VMEM((rows_n, SWA_OUT), F32),
        ],
        compiler_params=_params("arbitrary"),
        name="swa_sample",
    )(x, g, w_qkv, b_qkv, w_o, b_o, sink, bias, cache_k, cache_v)


def kernel(x_prompt, x_sample, state_ret, cache_swa_k, cache_swa_v, norm_mix, norm_mlp, norm_final, ab_w_in, ab_w_s, ab_b_s, ab_ln_g, ab_ln_b, ab_w_o, swa_w_qkv, swa_b_qkv, swa_sinks, swa_w_o, swa_b_o, mlp_w_up, mlp_w_down):
    batch, seq, _ = x_prompt.shape
    dec_batch, dec_len, _ = x_sample.shape
    assert seq % MIXER_ROWS == 0 and seq % RET_CHUNK == 0
    kv_lanes = SWA_KV_HEADS * SWA_HD

    hp = x_prompt.reshape(batch * seq, D_MODEL)
    hs = x_sample.reshape(dec_batch * dec_len, D_MODEL)
    row = lambda v: v.reshape(1, -1)
    gfin = row(norm_final)

    w_in, w_o = ab_w_in[0].astype(BF16), ab_w_o[0].astype(BF16)
    hp, ret_p, (w_up, w_dn) = _ab_prompt(hp, row(norm_mix[0]), w_in, w_o, ab_w_s[0], ab_b_s[0], ab_ln_g[0],
                                         ab_ln_b[0], batch, seq, cast=[(mlp_w_up, 0), (mlp_w_down, 0)])
    hs, ret_s, gm_s = _ab_sample(hs, row(norm_mix[0]), w_in, w_o, ab_w_s[0], ab_b_s[0], ab_ln_g[0], ab_ln_b[0],
                                 state_ret[0], dec_batch, dec_len)
    hp, (w_qkv, w_so, w_up1, w_dn1) = _mlp(hp, row(norm_mlp[0]), w_up, w_dn, gfin, final_norm=False,
                                           name="mlp0_prompt",
                                           cast=[(swa_w_qkv, 0), (swa_w_o, 0), (mlp_w_up, 1), (mlp_w_down, 1)])
    hs, _ = _mlp(hs, row(norm_mlp[0]), w_up, w_dn, gfin, final_norm=False, name="mlp0_sample")

    cache_t = lambda c: jnp.transpose(c[0], (0, 2, 3, 1)).reshape(dec_batch, kv_lanes, WINDOW)
    hp, kp, vp = _swa_prompt(hp, row(norm_mix[1]), w_qkv, row(swa_b_qkv[0]), swa_sinks[0], w_so, row(swa_b_o[0]),
                             batch, seq)
    hs, ks, vs = _swa_sample(hs, row(norm_mix[1]), w_qkv, row(swa_b_qkv[0]), swa_sinks[0], w_so, row(swa_b_o[0]),
                             cache_t(cache_swa_k), cache_t(cache_swa_v), dec_batch, dec_len)
    hp, _ = _mlp(hp, row(norm_mlp[1]), w_up1, w_dn1, gfin, final_norm=True, name="mlp1_prompt")
    hs, _ = _mlp(hs, row(norm_mlp[1]), w_up1, w_dn1, gfin, final_norm=True, name="mlp1_sample")

    cache_shape = lambda n: (1, n, WINDOW, SWA_KV_HEADS, SWA_HD)
    return (hp.reshape(batch, seq, D_MODEL),
            hs.reshape(dec_batch, dec_len, D_MODEL),
            ret_p[None],
            ret_s[None],
            gm_s.reshape(1, dec_batch, dec_len, GM_GROUPS * GM_GROUP_DIM),
            kp.reshape(cache_shape(batch)),
            vp.reshape(cache_shape(batch)),
            ks.reshape(cache_shape(dec_batch)),
            vs.reshape(cache_shape(dec_batch)))
```

```python
import functools

import numpy as np
import jax
import jax.numpy as jnp
from jax import lax
from jax.experimental import pallas as pl
from jax.experimental.pallas import tpu as pltpu

F32 = jnp.float32
BF16 = jnp.bfloat16

D_MODEL = 1024
PAST_LEN = 16384
RMS_EPS = 1e-6
LN_EPS = 1e-5
RET_HEADS = 4
RET_DK = 128
RET_DV = 128
RET_CHUNK = 128
ROPE_BASE = 10000.0
GM_GROUPS = 4
GM_GROUP_DIM = 128
GM_CHUNK = 128
AB_IN = 3072
AB_OUT = 1024
SWA_HEADS = 16
SWA_KV_HEADS = 2
SWA_HD = 64
WINDOW = 128
SWA_IN = 1280
SWA_OUT = 1024
D_FF = 4096

_Q0, _K0, _V0, _G0, _U0, _GV0 = 0, 512, 1024, 1536, 2048, 2560
_SK0, _SV0 = 1024, 1152

LANES = 128
MIXER_ROWS = 1024
MIXER_SPLIT = 4
MLP_ROWS = 1024
MLP_SPLIT = 4
FF_CHUNK = 1024
SAMPLE_BB = 16
VMEM_LIMIT = 56 * 1024 * 1024


def _rms(x, g):
    return x * lax.rsqrt(jnp.mean(x * x, axis=-1, keepdims=True) + RMS_EPS) * g


def _gelu(x):
    return 0.5 * x * (1.0 + lax.erf(x * np.float32(np.sqrt(0.5))))


def _silu(x):
    return x / (1.0 + jnp.exp(-x))


def _dot(a, b):
    return jnp.dot(a, b, preferred_element_type=F32)


def _dot_nt(a, b):
    return lax.dot_general(a, b, (((1,), (1,)), ((), ())), preferred_element_type=F32)


def _dot_tn(a, b):
    return lax.dot_general(a, b, (((0,), (0,)), ((), ())), preferred_element_type=F32)


def _const_spec(shape):
    zeros = (0,) * len(shape)
    return pl.BlockSpec(shape, lambda *_: zeros, pipeline_mode=pl.Buffered(1))


def _params(*sem):
    return pltpu.CompilerParams(dimension_semantics=sem, vmem_limit_bytes=VMEM_LIMIT)


def _cast_specs(weights, steps, flat_step):
    def src_map(layer):
        return lambda *g: (layer, flat_step(*g), 0)

    in_specs, out_specs, out_shapes = [], [], []
    for w, layer in weights:
        _, rows, cols = w.shape
        slab = rows // steps
        assert slab * steps == rows and slab % 16 == 0, (w.shape, steps)
        in_specs.append(pl.BlockSpec((None, slab, cols), src_map(layer)))
        out_specs.append(pl.BlockSpec((slab, cols), lambda *g: (flat_step(*g), 0)))
        out_shapes.append(jax.ShapeDtypeStruct((rows, cols), BF16))
    return in_specs, out_specs, out_shapes


def _cast_slabs(src_refs, dst_refs):
    for src, dst in zip(src_refs, dst_refs):
        dst[...] = src[...].astype(BF16)


def _mlp_kernel(x_ref, g_ref, wup_ref, wdn_ref, gf_ref, *rest, final_norm, n_cast):
    cast_src, o_ref, cast_dst = rest[:n_cast], rest[n_cast], rest[n_cast + 1:]
    _cast_slabs(cast_src, cast_dst)
    rows = x_ref.shape[0] // MLP_SPLIT
    groups = [slice(i * rows, (i + 1) * rows) for i in range(MLP_SPLIT)]
    xs = [x_ref[r, :] for r in groups]
    hns = [_rms(x, g_ref[...]).astype(BF16) for x in xs]
    accs = [jnp.zeros_like(x) for x in xs]
    for c in range(D_FF // FF_CHUNK):
        cols = slice(c * FF_CHUNK, (c + 1) * FF_CHUNK)
        for i in range(MLP_SPLIT):
            h = _dot(hns[i], wup_ref[:, cols])
            h = jnp.square(jnp.maximum(h, 0.0)).astype(BF16)
            accs[i] = accs[i] + _dot(h, wdn_ref[cols, :])
    for i, r in enumerate(groups):
        y = xs[i] + accs[i]
        if final_norm:
            y = _rms(y, gf_ref[...])
        o_ref[r, :] = y


def _mlp(x, g, wup, wdn, gf, *, final_norm, name, cast=()):
    rows = x.shape[0]
    tm = min(MLP_ROWS, rows)
    steps = rows // tm
    row_spec = pl.BlockSpec((tm, D_MODEL), lambda i: (i, 0))
    cast_in, cast_out, cast_shapes = _cast_specs(cast, steps, lambda i: i)
    out, *cast_w = pl.pallas_call(
        functools.partial(_mlp_kernel, final_norm=final_norm, n_cast=len(cast)),
        grid=(steps,),
        in_specs=[
            row_spec,
            _const_spec((1, D_MODEL)),
            _const_spec((D_MODEL, D_FF)),
            _const_spec((D_FF, D_MODEL)),
            _const_spec((1, D_MODEL)),
            *cast_in,
        ],
        out_specs=[row_spec, *cast_out],
        out_shape=[jax.ShapeDtypeStruct((rows, D_MODEL), F32), *cast_shapes],
        compiler_params=_params("arbitrary"),
        name=name,
    )(x, g, wup, wdn, gf, *(w for w, _ in cast))
    return out, cast_w


def _rotary(x, cos2, sin2):
    return x * cos2 + pltpu.roll(x, 64, 1) * sin2


def _retention_tables(chunk):
    log_g = np.log1p(-np.exp2(-5.0 - np.arange(RET_HEADS, dtype=np.float64)))
    idx = np.arange(chunk, dtype=np.float64)
    diff = idx[:, None] - idx[None, :]
    decay = np.where(diff[None] >= 0, np.exp(log_g[:, None, None] * np.maximum(diff, 0.0)[None]), 0.0)
    q_dec = np.exp(log_g[:, None] * (idx[None, :] + 1.0))
    k_dec = np.exp(log_g[:, None] * (chunk - 1.0 - idx[None, :]))
    chunk_dec = tuple(float(v) for v in np.exp(log_g * chunk))
    f32 = lambda t: t.astype(np.float32)
    return f32(decay), f32(q_dec), f32(k_dec), chunk_dec


def _rotary_tables(pos):
    half = RET_DK // 2
    inv = ROPE_BASE ** (-np.arange(half, dtype=np.float64) / half)
    ang = np.asarray(pos, np.float64)[:, None] * inv[None, :]
    cos, sin = np.cos(ang), np.sin(ang)
    cos2 = np.concatenate([cos, cos], axis=-1)
    sin2 = np.concatenate([-sin, sin], axis=-1)
    scale = RET_DK ** -0.5
    return np.concatenate([cos2 * scale, sin2 * scale, cos2, sin2], axis=-1).astype(np.float32)


def _gmlp_branch(z_ref, rows, lng_ref, lnb_ref, gi):
    u = _gelu(z_ref[rows, _U0 + gi * LANES:_U0 + (gi + 1) * LANES])
    gv = _gelu(z_ref[rows, _GV0 + gi * LANES:_GV0 + (gi + 1) * LANES])
    mu = jnp.mean(gv, axis=-1, keepdims=True)
    cen = gv - mu
    var = jnp.mean(cen * cen, axis=-1, keepdims=True)
    gvn = cen * lax.rsqrt(var + LN_EPS) * lng_ref[gi:gi + 1, :] + lnb_ref[gi:gi + 1, :]
    return u, gvn


def _ab_prompt_kernel(x_ref, g_ref, win_ref, wo_ref, rot_ref,
                      dec_ref, qd_ref, kd_ref, ws_ref, bs_ref, lng_ref, lnb_ref,
                      *rest, chunk_dec, n_cast):
    cast_src, (o_ref, st_ref), cast_dst = rest[:n_cast], rest[n_cast:n_cast + 2], rest[n_cast + 2:2 * n_cast + 2]
    scratch = rest[2 * n_cast + 2:]
    _cast_slabs(cast_src, cast_dst)
    si = pl.program_id(1)
    z_refs, cat_refs, s_ref = scratch[:MIXER_SPLIT], scratch[MIXER_SPLIT:2 * MIXER_SPLIT], scratch[-1]
    sub_rows = MIXER_ROWS // MIXER_SPLIT
    nchunk = sub_rows // RET_CHUNK

    @pl.when(si == 0)
    def _():
        s_ref[...] = jnp.zeros_like(s_ref)

    ri = lax.broadcasted_iota(jnp.int32, (GM_CHUNK, GM_CHUNK), 0)
    ci = lax.broadcasted_iota(jnp.int32, (GM_CHUNK, GM_CHUNK), 1)
    w_spatial = [jnp.where(ci <= ri, ws_ref[gi], 0.0).astype(BF16) for gi in range(GM_GROUPS)]
    tile_rows = lambda t: jnp.concatenate([t] * nchunk, axis=0)
    states = [s_ref[h] for h in range(RET_HEADS)]

    for sub in range(MIXER_SPLIT):
        rows = slice(sub * sub_rows, (sub + 1) * sub_rows)
        chunks = [slice(c * RET_CHUNK, (c + 1) * RET_CHUNK) for c in range(nchunk)]
        z_ref, cat_ref = z_refs[sub], cat_refs[sub]
        x = x_ref[rows, :]
        hn = _rms(x, g_ref[...]).astype(BF16)
        z_ref[:, :_U0] = _dot(hn, win_ref[:, :_U0])
        z_ref[:, _U0:] = _dot(hn, win_ref[:, _U0:])

        cq, sq, ck, sk = (rot_ref[rows, i * LANES:(i + 1) * LANES] for i in range(4))
        stage1 = []
        for h in range(RET_HEADS):
            q = _rotary(z_ref[:, _Q0 + h * LANES:_Q0 + (h + 1) * LANES], cq, sq)
            k = _rotary(z_ref[:, _K0 + h * LANES:_K0 + (h + 1) * LANES], ck, sk)
            v = z_ref[:, _V0 + h * LANES:_V0 + (h + 1) * LANES].astype(BF16)
            qb = q.astype(BF16)
            qdb = (q * tile_rows(qd_ref[h])).astype(BF16)
            local = [slice(c * RET_CHUNK, (c + 1) * RET_CHUNK) for c in range(nchunk)]
            kts = [k[r].T for r in local]
            scores = [_dot(qb[r], kt.astype(BF16)) for r, kt in zip(local, kts)]
            kvs = [_dot((kt * kd_ref[h]).astype(BF16), v[r]) for r, kt in zip(local, kts)]
            before = []
            for kv in kvs:
                before.append(states[h].astype(BF16))
                states[h] = states[h] * chunk_dec[h] + kv
            stage1.append((v, qdb, scores, before, local))
        for h, (v, qdb, scores, before, local) in enumerate(stage1):
            lanes = slice(h * LANES, (h + 1) * LANES)
            dec = dec_ref[h]
            for c, r in enumerate(local):
                lhs = jnp.concatenate([(scores[c] * dec).astype(BF16), qdb[r]], axis=1)
                rhs = jnp.concatenate([v[r], before[c]], axis=0)
                o = _dot(lhs, rhs)
                o = o * lax.rsqrt(jnp.mean(o * o, axis=-1, keepdims=True) + RMS_EPS)
                gate = _silu(z_ref[chunks[c], _G0 + h * LANES:_G0 + (h + 1) * LANES])
                cat_ref[chunks[c], lanes] = (gate * o).astype(BF16)

        for gi in range(GM_GROUPS):
            u, gvn = _gmlp_branch(z_ref, slice(None), lng_ref, lnb_ref, gi)
            gvb = gvn.astype(BF16)
            rhs = jnp.concatenate([gvb[c * GM_CHUNK:(c + 1) * GM_CHUNK] for c in range(nchunk)], axis=1)
            mixed = _dot(w_spatial[gi], rhs)
            bias = bs_ref[gi]
            for c in range(nchunk):
                local_rows = slice(c * GM_CHUNK, (c + 1) * GM_CHUNK)
                m = mixed[:, c * GM_CHUNK:(c + 1) * GM_CHUNK] + bias
                cat_ref[chunks[c], 512 + gi * LANES:512 + (gi + 1) * LANES] = (u[local_rows] * m).astype(BF16)

        o_ref[rows, :] = x + _dot(cat_ref[...], wo_ref[...])

    for h in range(RET_HEADS):
        s_ref[h] = states[h]

    @pl.when(si == pl.num_programs(1) - 1)
    def _():
        st_ref[...] = s_ref[...]


def _ab_prompt(x, g, w_in, w_o, w_s, b_s, ln_g, ln_b, batch, seq, cast=()):
    nstep = seq // MIXER_ROWS
    cast_in, cast_out, cast_shapes = _cast_specs(cast, batch * nstep, lambda b, s: b * nstep + s)
    decay, q_dec, k_dec, chunk_dec = _retention_tables(RET_CHUNK)
    rot = _rotary_tables(np.arange(seq))
    lane_b = lambda t: (np if isinstance(t, np.ndarray) else jnp).broadcast_to(t[:, :, None], t.shape + (LANES,))
    row_spec = pl.BlockSpec((MIXER_ROWS, D_MODEL), lambda b, s: (b * nstep + s, 0))
    rot_spec = pl.BlockSpec((MIXER_ROWS, 4 * LANES), lambda b, s: (s, 0))
    cube = (RET_HEADS, RET_CHUNK, LANES)
    out, state, *cast_w = pl.pallas_call(
        functools.partial(_ab_prompt_kernel, chunk_dec=chunk_dec, n_cast=len(cast)),
        grid=(batch, nstep),
        in_specs=[
            row_spec,
            _const_spec((1, D_MODEL)),
            _const_spec((D_MODEL, AB_IN)),
            _const_spec((AB_OUT, D_MODEL)),
            rot_spec,
            _const_spec(cube), _const_spec(cube), _const_spec(cube),
            _const_spec(cube), _const_spec(cube),
            _const_spec((GM_GROUPS, GM_GROUP_DIM)), _const_spec((GM_GROUPS, GM_GROUP_DIM)),
            *cast_in,
        ],
        out_specs=[
            row_spec,
            pl.BlockSpec((None, RET_HEADS, RET_DK, RET_DV), lambda b, s: (b, 0, 0, 0)),
            *cast_out,
        ],
        out_shape=[
            jax.ShapeDtypeStruct((batch * seq, D_MODEL), F32),
            jax.ShapeDtypeStruct((batch, RET_HEADS, RET_DK, RET_DV), F32),
            *cast_shapes,
        ],
        scratch_shapes=(
            [pltpu.VMEM((MIXER_ROWS // MIXER_SPLIT, AB_IN), F32)] * MIXER_SPLIT
            + [pltpu.VMEM((MIXER_ROWS // MIXER_SPLIT, AB_OUT), BF16)] * MIXER_SPLIT
            + [pltpu.VMEM((RET_HEADS, RET_DK, RET_DV), F32)]),
        compiler_params=_params("arbitrary", "arbitrary"),
        name="ab_prompt",
    )(x, g, w_in, w_o, rot, decay, lane_b(q_dec), np.broadcast_to(k_dec[:, None, :], cube),
      w_s, lane_b(b_s), ln_g, ln_b, *(w for w, _ in cast))
    return out, state, cast_w


def _ab_sample_kernel(x_ref, g_ref, win_ref, wo_ref, rot_ref,
                      dec_ref, qd_ref, kd_ref, ws_ref, bs_ref, lng_ref, lnb_ref, causal_ref, st_ref,
                      o_ref, sto_ref, gvn_ref, z_ref, cat_ref, *, chunk_dec, dec_len):
    rows_n = SAMPLE_BB * dec_len
    x = x_ref[...]
    hn = _rms(x, g_ref[...]).astype(BF16)
    z_ref[...] = _dot(hn, win_ref[...])

    cq, sq, ck, sk = (rot_ref[:, i * LANES:(i + 1) * LANES] for i in range(4))
    token = lax.broadcasted_iota(jnp.int32, (rows_n, rows_n), 1)
    for h in range(RET_HEADS):
        lanes = slice(h * LANES, (h + 1) * LANES)
        q = _rotary(z_ref[:, _Q0 + h * LANES:_Q0 + (h + 1) * LANES], cq, sq)
        k = _rotary(z_ref[:, _K0 + h * LANES:_K0 + (h + 1) * LANES], ck, sk)
        v = z_ref[:, _V0 + h * LANES:_V0 + (h + 1) * LANES].astype(BF16)
        scores = _dot_nt(q.astype(BF16), k.astype(BF16)) * dec_ref[h]
        intra = _dot(scores.astype(BF16), v)
        qs = q * qd_ref[h]
        kst = (k * kd_ref[h]).T
        kst_all = jnp.concatenate(
            [jnp.where((token >= b * dec_len) & (token < (b + 1) * dec_len), kst, 0.0).astype(BF16)
             for b in range(SAMPLE_BB)], axis=0)
        kv_all = _dot(kst_all, v)
        cross = []
        for b in range(SAMPLE_BB):
            rows = slice(b * dec_len, (b + 1) * dec_len)
            state = st_ref[b, h]
            cross.append(_dot(qs[rows].astype(BF16), state.astype(BF16)))
            sto_ref[b, h] = state * chunk_dec[h] + kv_all[b * RET_DK:(b + 1) * RET_DK]
        o = intra + jnp.concatenate(cross, axis=0)
        o = o * lax.rsqrt(jnp.mean(o * o, axis=-1, keepdims=True) + RMS_EPS)
        gate = _silu(z_ref[:, _G0 + h * LANES:_G0 + (h + 1) * LANES])
        cat_ref[:, lanes] = (gate * o).astype(BF16)

    causal = causal_ref[...] > 0.0
    for gi in range(GM_GROUPS):
        u, gvn = _gmlp_branch(z_ref, slice(None), lng_ref, lnb_ref, gi)
        gvn_ref[:, gi * LANES:(gi + 1) * LANES] = gvn
        w = jnp.where(causal, ws_ref[gi], 0.0).astype(BF16)
        mixed = _dot(w, gvn.astype(BF16)) + bs_ref[gi]
        cat_ref[:, 512 + gi * LANES:512 + (gi + 1) * LANES] = (u * mixed).astype(BF16)

    o_ref[...] = x + _dot(cat_ref[...], wo_ref[...])


def _ab_sample(x, g, w_in, w_o, w_s, b_s, ln_g, ln_b, state, dec_batch, dec_len):
    rows_n = SAMPLE_BB * dec_len
    assert rows_n == LANES and dec_batch % SAMPLE_BB == 0
    decay, q_dec, k_dec, chunk_dec = _retention_tables(dec_len)
    rot = np.tile(_rotary_tables(PAST_LEN + np.arange(dec_len)), (SAMPLE_BB, 1))
    bd_decay = np.stack([np.kron(np.eye(SAMPLE_BB, dtype=np.float32), d) for d in decay])

    def rows_b(t):
        xp = np if isinstance(t, np.ndarray) else jnp
        return xp.broadcast_to(xp.tile(t, (1, SAMPLE_BB))[:, :, None], (t.shape[0], rows_n, LANES))
    w_tiled = jnp.tile(w_s[:, :dec_len, :dec_len], (1, SAMPLE_BB, SAMPLE_BB))
    causal = jnp.asarray(np.kron(np.eye(SAMPLE_BB), np.tril(np.ones((dec_len, dec_len)))), F32)
    row_spec = pl.BlockSpec((rows_n, D_MODEL), lambda i: (i, 0))
    st_spec = pl.BlockSpec((SAMPLE_BB, RET_HEADS, RET_DK, RET_DV), lambda i: (i, 0, 0, 0))
    cube = (RET_HEADS, rows_n, LANES)
    tab = (rows_n, LANES)
    return pl.pallas_call(
        functools.partial(_ab_sample_kernel, chunk_dec=chunk_dec, dec_len=dec_len),
        grid=(dec_batch // SAMPLE_BB,),
        in_specs=[
            row_spec,
            _const_spec((1, D_MODEL)),
            _const_spec((D_MODEL, AB_IN)),
            _const_spec((AB_OUT, D_MODEL)),
            _const_spec((rows_n, 4 * LANES)),
            _const_spec(cube), _const_spec(cube), _const_spec(cube),
            _const_spec(cube), _const_spec(cube),
            _const_spec((GM_GROUPS, GM_GROUP_DIM)), _const_spec((GM_GROUPS, GM_GROUP_DIM)),
            _const_spec(tab),
            st_spec,
        ],
        out_specs=[
            row_spec,
            st_spec,
            pl.BlockSpec((rows_n, GM_GROUPS * GM_GROUP_DIM), lambda i: (i, 0)),
        ],
        out_shape=[
            jax.ShapeDtypeStruct((dec_batch * dec_len, D_MODEL), F32),
            jax.ShapeDtypeStruct(state.shape, F32),
            jax.ShapeDtypeStruct((dec_batch * dec_len, GM_GROUPS * GM_GROUP_DIM), F32),
        ],
        scratch_shapes=[
            pltpu.VMEM((rows_n, AB_IN), F32),
            pltpu.VMEM((rows_n, AB_OUT), BF16),
        ],
        compiler_params=_params("arbitrary"),
        name="ab_sample",
    )(x, g, w_in, w_o, rot, bd_decay, rows_b(q_dec), rows_b(k_dec),
      w_tiled, rows_b(b_s[:, :dec_len]), ln_g, ln_b, causal, state)


_LOG2E = float(np.log2(np.e))


def _softmax_numerators(s, sink_even, sink_odd):
    s0, s1, s2, s3 = (s[:, i * LANES:(i + 1) * LANES] for i in range(4))
    m_even = jnp.max(jnp.maximum(jnp.maximum(s0, s1), sink_even), axis=-1, keepdims=True)
    m_odd = jnp.max(jnp.maximum(jnp.maximum(s2, s3), sink_odd), axis=-1, keepdims=True)
    p = jnp.concatenate([jnp.exp2(s0 - m_even), jnp.exp2(s1 - m_even),
                         jnp.exp2(s2 - m_odd), jnp.exp2(s3 - m_odd)], axis=1)
    even_lane = lax.broadcasted_iota(jnp.int32, sink_even.shape, 1) < SWA_HD
    sink_term = jnp.where(even_lane, jnp.exp2(sink_even - m_even), jnp.exp2(sink_odd - m_odd))
    return p.astype(BF16), sink_term


def _band_bias(q_rows, first_key_row):
    i = np.arange(q_rows)[:, None]
    j = np.arange(2 * WINDOW)[None, :]
    rel = i + first_key_row - j
    half = np.where((rel >= 0) & (rel <= WINDOW), 0.0, -np.inf).astype(np.float32)
    return np.concatenate([half, half], axis=1)


def _kv_layouts(kcol, vcol):
    even_lane = lax.broadcasted_iota(jnp.int32, kcol.shape, 1) < SWA_HD
    ks = kcol * (SWA_HD ** -0.5 * _LOG2E)
    kr = pltpu.roll(ks, SWA_HD, 1)
    vr = pltpu.roll(vcol, SWA_HD, 1)
    one_even = jnp.where(even_lane, 1.0, 0.0)
    one_odd = 1.0 - one_even
    zero = jnp.zeros_like(kcol)
    out = []
    for kvh in range(SWA_KV_HEADS):
        k_lo, k_hi = (ks, kr) if kvh == 0 else (kr, ks)
        v_lo, v_hi = (vcol, vr) if kvh == 0 else (vr, vcol)
        ktop = jnp.where(even_lane, k_lo, zero).astype(BF16)
        kbot = jnp.where(even_lane, zero, k_hi).astype(BF16)
        vtop = jnp.concatenate([jnp.where(even_lane, v_lo, zero), one_even], axis=1).astype(BF16)
        vbot = jnp.concatenate([jnp.where(even_lane, zero, v_hi), one_odd], axis=1).astype(BF16)
        out.append((ktop, kbot, vtop, vbot))
    return out


def _swa_prompt_kernel(x_ref, xh_ref, g_ref, wqkv_ref, bqkv_ref, wo_ref, bo_ref, sink_ref, bias_ref,
                       o_ref, kc_ref, vc_ref, z_ref, cat_ref):
    si = pl.program_id(1)
    nblk = MIXER_ROWS // WINDOW
    npair = SWA_HEADS // SWA_KV_HEADS // 2
    sub_rows = MIXER_ROWS // MIXER_SPLIT

    x = x_ref[...]
    hn = _rms(x, g_ref[...]).astype(BF16)
    for sub in range(MIXER_SPLIT):
        rows = slice(sub * sub_rows, (sub + 1) * sub_rows)
        z_ref[rows, :] = _dot(hn[rows], wqkv_ref[...]) + bqkv_ref[...]
    hh = _rms(xh_ref[...], g_ref[...]).astype(BF16)
    zh = _dot(hh, wqkv_ref[:, _SK0:]) + bqkv_ref[:, _SK0:]

    kcol = z_ref[:, _SK0:_SK0 + LANES]
    vcol = z_ref[:, _SV0:_SV0 + LANES]
    kc_ref[...] = kcol[MIXER_ROWS - WINDOW:].T
    vc_ref[...] = vcol[MIXER_ROWS - WINDOW:].T
    layouts = _kv_layouts(jnp.concatenate([zh[:, :LANES], kcol], axis=0),
                          jnp.concatenate([zh[:, LANES:], vcol], axis=0))

    sink_rows = lambda tab: jnp.concatenate(
        [tab[p * 8:(p + 1) * 8] for p in range(npair) for _ in range(WINDOW // 8)], axis=0)
    sinks = [[sink_rows(sink_ref[kvh, par]) for par in range(2)] for kvh in range(SWA_KV_HEADS)]

    for r in range(nblk):
        rows = slice(r * WINDOW, (r + 1) * WINDOW)
        keys = slice(r * WINDOW, (r + 2) * WINDOW)
        tile = bias_ref[jnp.where(si > 0, 0, 1)] if r == 0 else bias_ref[0]
        bias = jnp.concatenate([tile] * npair, axis=0)
        for kvh, (ktop, kbot, vtop, vbot) in enumerate(layouts):
            c0 = kvh * npair * LANES
            lhs = jnp.concatenate(
                [z_ref[rows, c0 + p * LANES:c0 + (p + 1) * LANES] for p in range(npair)], axis=0).astype(BF16)
            s = _dot_nt(lhs, jnp.concatenate([ktop[keys], kbot[keys]], axis=0)) + bias
            p, sink_term = _softmax_numerators(s, sinks[kvh][0], sinks[kvh][1])
            out = _dot(p, jnp.concatenate([vtop[keys], vbot[keys]], axis=0))
            o = out[:, :LANES] / (out[:, LANES:] + sink_term)
            for p_i in range(npair):
                cat_ref[rows, c0 + p_i * LANES:c0 + (p_i + 1) * LANES] = (
                    o[p_i * WINDOW:(p_i + 1) * WINDOW].astype(BF16))

    o_ref[...] = x + _dot(cat_ref[...], wo_ref[...]) + bo_ref[...]


def _sink_table(sinks, rows_per_pair):
    npair = SWA_HEADS // SWA_KV_HEADS // 2
    s = (sinks.astype(F32) * _LOG2E).reshape(SWA_KV_HEADS, npair, 2).transpose(0, 2, 1)[:, :, :, None, None]
    s = jnp.broadcast_to(s, (SWA_KV_HEADS, 2, npair, rows_per_pair, LANES))
    return s.reshape(SWA_KV_HEADS, 2, npair * rows_per_pair, LANES)


def _swa_prompt(x, g, w_qkv, b_qkv, sinks, w_o, b_o, batch, seq):
    nstep = seq // MIXER_ROWS
    npair = SWA_HEADS // SWA_KV_HEADS // 2
    row_spec = pl.BlockSpec((MIXER_ROWS, D_MODEL), lambda b, s: (b * nstep + s, 0))
    nblk = MIXER_ROWS // WINDOW
    halo_spec = pl.BlockSpec((WINDOW, D_MODEL), lambda b, s: (jnp.maximum((b * nstep + s) * nblk - 1, 0), 0))
    cache_spec = pl.BlockSpec((None, WINDOW, LANES), lambda b, s: (b, 0, 0))
    band = _band_bias(WINDOW, WINDOW)
    no_prev = np.where(np.arange(4 * WINDOW)[None, :] % (2 * WINDOW) >= WINDOW, band, -np.inf).astype(np.float32)
    bias = jnp.asarray(np.stack([band, no_prev]))
    return pl.pallas_call(
        _swa_prompt_kernel,
        grid=(batch, nstep),
        in_specs=[
            row_spec,
            halo_spec,
            _const_spec((1, D_MODEL)),
            _const_spec((D_MODEL, SWA_IN)),
            _const_spec((1, SWA_IN)),
            _const_spec((SWA_OUT, D_MODEL)),
            _const_spec((1, D_MODEL)),
            _const_spec((SWA_KV_HEADS, 2, npair * 8, LANES)),
            _const_spec((2, WINDOW, 4 * WINDOW)),
        ],
        out_specs=[row_spec, cache_spec, cache_spec],
        out_shape=[
            jax.ShapeDtypeStruct((batch * seq, D_MODEL), F32),
            jax.ShapeDtypeStruct((batch, WINDOW, LANES), F32),
            jax.ShapeDtypeStruct((batch, WINDOW, LANES), F32),
        ],
        scratch_shapes=[
            pltpu.VMEM((MIXER_ROWS, SWA_IN), F32),
            pltpu.VMEM((MIXER_ROWS, SWA_OUT), BF16),
        ],
        compiler_params=_params("arbitrary", "arbitrary"),
        name="swa_prompt",
    )(x, x, g, w_qkv, b_qkv, w_o, b_o, _sink_table(sinks, 8), bias)


def _swa_sample_kernel(x_ref, g_ref, wqkv_ref, bqkv_ref, wo_ref, bo_ref, sink_ref, bias_ref, ck_ref, cv_ref,
                       o_ref, nk_ref, nv_ref, z_ref, cat_ref, *, dec_len):
    ncol = SWA_HEADS // 2
    grp = ncol * dec_len
    x = x_ref[...]
    hn = _rms(x, g_ref[...]).astype(BF16)
    z_ref[...] = _dot(hn, wqkv_ref[...]) + bqkv_ref[...]

    bias, sink = bias_ref[...], sink_ref[...]
    lane_lo = lax.broadcasted_iota(jnp.int32, (grp, LANES), 1) < SWA_HD
    kv0 = lax.broadcasted_iota(jnp.int32, (grp, LANES), 0) < grp // 2
    own_half = jnp.where(kv0, 0, 1) == jnp.where(lane_lo, 0, 1)
    pad = jnp.zeros((WINDOW - dec_len, LANES), F32)
    elems = [slice(b * dec_len, (b + 1) * dec_len) for b in range(SAMPLE_BB)]
    values, scores = [], []
    for b, rows in enumerate(elems):
        knew = z_ref[rows, _SK0:_SK0 + LANES]
        vnew = z_ref[rows, _SV0:_SV0 + LANES]
        kold, vold = ck_ref[b].T, cv_ref[b].T
        nk_ref[b, 0:WINDOW - dec_len, :] = kold[dec_len:]
        nk_ref[b, WINDOW - dec_len:, :] = knew
        nv_ref[b, 0:WINDOW - dec_len, :] = vold[dec_len:]
        nv_ref[b, WINDOW - dec_len:, :] = vnew
        kall = jnp.concatenate([kold, knew, pad], axis=0).astype(BF16)
        values.append(jnp.concatenate([vold, vnew, pad], axis=0).astype(BF16))

        q = jnp.concatenate([z_ref[rows, j * LANES:(j + 1) * LANES] for j in range(ncol)], axis=0)
        q = q * (SWA_HD ** -0.5 * _LOG2E)
        q_swapped = pltpu.roll(q, SWA_HD, 1)
        q_even = jnp.where(own_half, jnp.where(kv0, q, q_swapped), 0.0)
        q_odd = jnp.where(own_half, jnp.where(kv0, q_swapped, q), 0.0)
        lhs = jnp.concatenate([q_even, q_odd], axis=0).astype(BF16)
        scores.append(_dot_nt(lhs, kall) + bias)

    probs, dens = [], []
    for s in scores:
        m = jnp.max(jnp.maximum(jnp.maximum(s[:, :LANES], s[:, LANES:]), sink), axis=-1, keepdims=True)
        p = jnp.exp2(s - m)
        dens.append(jnp.sum(p, axis=-1, keepdims=True) + jnp.exp2(sink - m))
        probs.append(p.astype(BF16))

    outs = [_dot(p, v) for p, v in zip(probs, values)]

    for rows, out, den in zip(elems, outs, dens):
        o = out / den
        o_swapped = pltpu.roll(o, SWA_HD, 1)
        cols = jnp.where(lane_lo,
                         jnp.where(kv0, o[:grp], o_swapped[:grp]),
                         jnp.where(kv0, o_swapped[grp:], o[grp:]))
        for j in range(ncol):
            cat_ref[rows, j * LANES:(j + 1) * LANES] = cols[j * dec_len:(j + 1) * dec_len]

    o_ref[...] = x + _dot(cat_ref[...].astype(BF16), wo_ref[...]) + bo_ref[...]


def _swa_sample(x, g, w_qkv, b_qkv, sinks, w_o, b_o, cache_k, cache_v, dec_batch, dec_len):
    rows_n = SAMPLE_BB * dec_len
    ncol = SWA_HEADS // 2
    score_rows = 2 * ncol * dec_len
    sink = jnp.broadcast_to((sinks.astype(F32) * _LOG2E).reshape(ncol, 2).T[:, :, None, None],
                            (2, ncol, dec_len, LANES)).reshape(score_rows, LANES)
    bias = jnp.asarray(np.tile(_band_bias(dec_len, WINDOW)[:, :2 * WINDOW], (2 * ncol, 1)))
    row_spec = pl.BlockSpec((rows_n, D_MODEL), lambda i: (i, 0))
    cache_spec = pl.BlockSpec((SAMPLE_BB, WINDOW, LANES), lambda i: (i, 0, 0))
    return pl.pallas_call(
        functools.partial(_swa_sample_kernel, dec_len=dec_len),
        grid=(dec_batch // SAMPLE_BB,),
        in_specs=[
            row_spec,
            _const_spec((1, D_MODEL)),
            _const_spec((D_MODEL, SWA_IN)),
            _const_spec((1, SWA_IN)),
            _const_spec((SWA_OUT, D_MODEL)),
            _const_spec((1, D_MODEL)),
            _const_spec((score_rows, LANES)),
            _const_spec((score_rows, 2 * WINDOW)),
            cache_spec, cache_spec,
        ],
        out_specs=[row_spec, cache_spec, cache_spec],
        out_shape=[
            jax.ShapeDtypeStruct((dec_batch * dec_len, D_MODEL), F32),
            jax.ShapeDtypeStruct((dec_batch, WINDOW, LANES), F32),
            jax.ShapeDtypeStruct((dec_batch, WINDOW, LANES), F32),
        ],
        scratch_shapes=[
            pltpu.VMEM((rows_n, SWA_IN), F32),
            pltpu.VMEM((rows_n, SWA_OUT), F32),
        ],
        compiler_params=_params("arbitrary"),
        name="swa_sample",
    )(x, g, w_qkv, b_qkv, w_o, b_o, sink, bias, cache_k, cache_v)


def kernel(x_prompt, x_sample, state_ret, cache_swa_k, cache_swa_v, norm_mix, norm_mlp, norm_final, ab_w_in, ab_w_s, ab_b_s, ab_ln_g, ab_ln_b, ab_w_o, swa_w_qkv, swa_b_qkv, swa_sinks, swa_w_o, swa_b_o, mlp_w_up, mlp_w_down):
    batch, seq, _ = x_prompt.shape
    dec_batch, dec_len, _ = x_sample.shape
    assert seq % MIXER_ROWS == 0 and seq % RET_CHUNK == 0
    kv_lanes = SWA_KV_HEADS * SWA_HD

    hp = x_prompt.reshape(batch * seq, D_MODEL)
    hs = x_sample.reshape(dec_batch * dec_len, D_MODEL)
    row = lambda v: v.reshape(1, -1)
    gfin = row(norm_final)

    w_in, w_o = ab_w_in[0].astype(BF16), ab_w_o[0].astype(BF16)
    hp, ret_p, (w_up, w_dn) = _ab_prompt(hp, row(norm_mix[0]), w_in, w_o, ab_w_s[0], ab_b_s[0], ab_ln_g[0],
                                         ab_ln_b[0], batch, seq, cast=[(mlp_w_up, 0), (mlp_w_down, 0)])
    hs, ret_s, gm_s = _ab_sample(hs, row(norm_mix[0]), w_in, w_o, ab_w_s[0], ab_b_s[0], ab_ln_g[0], ab_ln_b[0],
                                 state_ret[0], dec_batch, dec_len)
    hp, (w_qkv, w_so, w_up1, w_dn1) = _mlp(hp, row(norm_mlp[0]), w_up, w_dn, gfin, final_norm=False,
                                           name="mlp0_prompt",
                                           cast=[(swa_w_qkv, 0), (swa_w_o, 0), (mlp_w_up, 1), (mlp_w_down, 1)])
    hs, _ = _mlp(hs, row(norm_mlp[0]), w_up, w_dn, gfin, final_norm=False, name="mlp0_sample")

    cache_t = lambda c: jnp.transpose(c[0], (0, 2, 3, 1)).reshape(dec_batch, kv_lanes, WINDOW)
    hp, kp, vp = _swa_prompt(hp, row(norm_mix[1]), w_qkv, row(swa_b_qkv[0]), swa_sinks[0], w_so, row(swa_b_o[0]),
                             batch, seq)
    hs, ks, vs = _swa_sample(hs, row(norm_mix[1]), w_qkv, row(swa_b_qkv[0]), swa_sinks[0], w_so, row(swa_b_o[0]),
                             cache_t(cache_swa_k), cache_t(cache_swa_v), dec_batch, dec_len)
    hp, _ = _mlp(hp, row(norm_mlp[1]), w_up1, w_dn1, gfin, final_norm=True, name="mlp1_prompt")
    hs, _ = _mlp(hs, row(norm_mlp[1]), w_up1, w_dn1, gfin, final_norm=True, name="mlp1_sample")

    cache_shape = lambda n: (1, n, WINDOW, SWA_KV_HEADS, SWA_HD)
    prompt_cache = lambda c: jnp.transpose(c.reshape(batch, SWA_KV_HEADS, SWA_HD, WINDOW), (0, 3, 1, 2))[None]
    return (hp.reshape(batch, seq, D_MODEL),
            hs.reshape(dec_batch, dec_len, D_MODEL),
            ret_p[None],
            ret_s[None],
            gm_s.reshape(1, dec_batch, dec_len, GM_GROUPS * GM_GROUP_DIM),
            prompt_cache(kp),
            prompt_cache(vp),
            ks.reshape(cache_shape(dec_batch)),
            vs.reshape(cache_shape(dec_batch)))
```

```python
import functools

import numpy as np
import jax
import jax.numpy as jnp
from jax import lax
from jax.experimental import pallas as pl
from jax.experimental.pallas import tpu as pltpu

F32 = jnp.float32
BF16 = jnp.bfloat16

D_MODEL = 1024
PAST_LEN = 16384
RMS_EPS = 1e-6
LN_EPS = 1e-5
RET_HEADS = 4
RET_DK = 128
RET_DV = 128
RET_CHUNK = 128
ROPE_BASE = 10000.0
GM_GROUPS = 4
GM_GROUP_DIM = 128
GM_CHUNK = 128
AB_IN = 3072
AB_OUT = 1024
SWA_HEADS = 16
SWA_KV_HEADS = 2
SWA_HD = 64
WINDOW = 128
SWA_IN = 1280
SWA_OUT = 1024
D_FF = 4096

_Q0, _K0, _V0, _G0, _U0, _GV0 = 0, 512, 1024, 1536, 2048, 2560
_SK0, _SV0 = 1024, 1152

LANES = 128
MIXER_ROWS = 1024
MIXER_SPLIT = 4
MLP_ROWS = 1024
MLP_SPLIT = 4
FF_CHUNK = 1024
SAMPLE_BB = 16
VMEM_LIMIT = 56 * 1024 * 1024


def _rms(x, g):
    return x * lax.rsqrt(jnp.mean(x * x, axis=-1, keepdims=True) + RMS_EPS) * g


def _gelu(x):
    return 0.5 * x * (1.0 + lax.erf(x * np.float32(np.sqrt(0.5))))


def _silu(x):
    return x / (1.0 + jnp.exp(-x))


def _dot(a, b):
    return jnp.dot(a, b, preferred_element_type=F32)


def _dot_nt(a, b):
    return lax.dot_general(a, b, (((1,), (1,)), ((), ())), preferred_element_type=F32)


def _dot_tn(a, b):
    return lax.dot_general(a, b, (((0,), (0,)), ((), ())), preferred_element_type=F32)


def _const_spec(shape):
    zeros = (0,) * len(shape)
    return pl.BlockSpec(shape, lambda *_: zeros, pipeline_mode=pl.Buffered(1))


def _params(*sem):
    return pltpu.CompilerParams(dimension_semantics=sem, vmem_limit_bytes=VMEM_LIMIT)


def _cast_specs(weights, steps, flat_step):
    def src_map(layer):
        return lambda *g: (layer, flat_step(*g), 0)

    in_specs, out_specs, out_shapes = [], [], []
    for w, layer in weights:
        _, rows, cols = w.shape
        slab = rows // steps
        assert slab * steps == rows and slab % 16 == 0, (w.shape, steps)
        in_specs.append(pl.BlockSpec((None, slab, cols), src_map(layer)))
        out_specs.append(pl.BlockSpec((slab, cols), lambda *g: (flat_step(*g), 0)))
        out_shapes.append(jax.ShapeDtypeStruct((rows, cols), BF16))
    return in_specs, out_specs, out_shapes


def _cast_slabs(src_refs, dst_refs):
    for src, dst in zip(src_refs, dst_refs):
        dst[...] = src[...].astype(BF16)


def _mlp_kernel(x_ref, g_ref, wup_ref, wdn_ref, gf_ref, *rest, final_norm, n_cast):
    cast_src, o_ref, cast_dst = rest[:n_cast], rest[n_cast], rest[n_cast + 1:]
    _cast_slabs(cast_src, cast_dst)
    rows = x_ref.shape[0] // MLP_SPLIT
    groups = [slice(i * rows, (i + 1) * rows) for i in range(MLP_SPLIT)]
    xs = [x_ref[r, :] for r in groups]
    hns = [_rms(x, g_ref[...]).astype(BF16) for x in xs]
    accs = [jnp.zeros_like(x) for x in xs]
    for c in range(D_FF // FF_CHUNK):
        cols = slice(c * FF_CHUNK, (c + 1) * FF_CHUNK)
        for i in range(MLP_SPLIT):
            h = _dot(hns[i], wup_ref[:, cols])
            h = jnp.square(jnp.maximum(h, 0.0)).astype(BF16)
            accs[i] = accs[i] + _dot(h, wdn_ref[cols, :])
    for i, r in enumerate(groups):
        y = xs[i] + accs[i]
        if final_norm:
            y = _rms(y, gf_ref[...])
        o_ref[r, :] = y


def _mlp(x, g, wup, wdn, gf, *, final_norm, name, cast=()):
    rows = x.shape[0]
    tm = min(MLP_ROWS, rows)
    steps = rows // tm
    row_spec = pl.BlockSpec((tm, D_MODEL), lambda i: (i, 0))
    cast_in, cast_out, cast_shapes = _cast_specs(cast, steps, lambda i: i)
    out, *cast_w = pl.pallas_call(
        functools.partial(_mlp_kernel, final_norm=final_norm, n_cast=len(cast)),
        grid=(steps,),
        in_specs=[
            row_spec,
            _const_spec((1, D_MODEL)),
            _const_spec((D_MODEL, D_FF)),
            _const_spec((D_FF, D_MODEL)),
            _const_spec((1, D_MODEL)),
            *cast_in,
        ],
        out_specs=[row_spec, *cast_out],
        out_shape=[jax.ShapeDtypeStruct((rows, D_MODEL), F32), *cast_shapes],
        compiler_params=_params("arbitrary"),
        name=name,
    )(x, g, wup, wdn, gf, *(w for w, _ in cast))
    return out, cast_w


def _rotary(x, cos2, sin2):
    return x * cos2 + pltpu.roll(x, 64, 1) * sin2


def _retention_tables(chunk):
    log_g = np.log1p(-np.exp2(-5.0 - np.arange(RET_HEADS, dtype=np.float64)))
    idx = np.arange(chunk, dtype=np.float64)
    diff = idx[:, None] - idx[None, :]
    decay = np.where(diff[None] >= 0, np.exp(log_g[:, None, None] * np.maximum(diff, 0.0)[None]), 0.0)
    q_dec = np.exp(log_g[:, None] * (idx[None, :] + 1.0))
    k_dec = np.exp(log_g[:, None] * (chunk - 1.0 - idx[None, :]))
    chunk_dec = tuple(float(v) for v in np.exp(log_g * chunk))
    f32 = lambda t: t.astype(np.float32)
    return f32(decay), f32(q_dec), f32(k_dec), chunk_dec


def _rotary_tables(pos):
    half = RET_DK // 2
    inv = ROPE_BASE ** (-np.arange(half, dtype=np.float64) / half)
    ang = np.asarray(pos, np.float64)[:, None] * inv[None, :]
    cos, sin = np.cos(ang), np.sin(ang)
    cos2 = np.concatenate([cos, cos], axis=-1)
    sin2 = np.concatenate([-sin, sin], axis=-1)
    scale = RET_DK ** -0.5
    return np.concatenate([cos2 * scale, sin2 * scale, cos2, sin2], axis=-1).astype(np.float32)


def _gmlp_branch(z_ref, rows, lng_ref, lnb_ref, gi):
    u = _gelu(z_ref[rows, _U0 + gi * LANES:_U0 + (gi + 1) * LANES])
    gv = _gelu(z_ref[rows, _GV0 + gi * LANES:_GV0 + (gi + 1) * LANES])
    mu = jnp.mean(gv, axis=-1, keepdims=True)
    cen = gv - mu
    var = jnp.mean(cen * cen, axis=-1, keepdims=True)
    gvn = cen * lax.rsqrt(var + LN_EPS) * lng_ref[gi:gi + 1, :] + lnb_ref[gi:gi + 1, :]
    return u, gvn


def _ab_prompt_kernel(x_ref, g_ref, win_ref, wo_ref, rot_ref,
                      dec_ref, qd_ref, kd_ref, ws_ref, bs_ref, lng_ref, lnb_ref,
                      *rest, chunk_dec, n_cast):
    cast_src, (o_ref, st_ref), cast_dst = rest[:n_cast], rest[n_cast:n_cast + 2], rest[n_cast + 2:2 * n_cast + 2]
    scratch = rest[2 * n_cast + 2:]
    _cast_slabs(cast_src, cast_dst)
    si = pl.program_id(1)
    z_refs, cat_refs, s_ref = scratch[:MIXER_SPLIT], scratch[MIXER_SPLIT:2 * MIXER_SPLIT], scratch[-1]
    sub_rows = MIXER_ROWS // MIXER_SPLIT
    nchunk = sub_rows // RET_CHUNK

    @pl.when(si == 0)
    def _():
        s_ref[...] = jnp.zeros_like(s_ref)

    ri = lax.broadcasted_iota(jnp.int32, (GM_CHUNK, GM_CHUNK), 0)
    ci = lax.broadcasted_iota(jnp.int32, (GM_CHUNK, GM_CHUNK), 1)
    w_spatial = [jnp.where(ci <= ri, ws_ref[gi], 0.0).astype(BF16) for gi in range(GM_GROUPS)]
    tile_rows = lambda t: jnp.concatenate([t] * nchunk, axis=0)
    states = [s_ref[h] for h in range(RET_HEADS)]

    for sub in range(MIXER_SPLIT):
        rows = slice(sub * sub_rows, (sub + 1) * sub_rows)
        chunks = [slice(c * RET_CHUNK, (c + 1) * RET_CHUNK) for c in range(nchunk)]
        z_ref, cat_ref = z_refs[sub], cat_refs[sub]
        x = x_ref[rows, :]
        hn = _rms(x, g_ref[...]).astype(BF16)
        z_ref[:, :_U0] = _dot(hn, win_ref[:, :_U0])
        z_ref[:, _U0:] = _dot(hn, win_ref[:, _U0:])

        cq, sq, ck, sk = (rot_ref[rows, i * LANES:(i + 1) * LANES] for i in range(4))
        stage1 = []
        for h in range(RET_HEADS):
            q = _rotary(z_ref[:, _Q0 + h * LANES:_Q0 + (h + 1) * LANES], cq, sq)
            k = _rotary(z_ref[:, _K0 + h * LANES:_K0 + (h + 1) * LANES], ck, sk)
            v = z_ref[:, _V0 + h * LANES:_V0 + (h + 1) * LANES].astype(BF16)
            qb = q.astype(BF16)
            qdb = (q * tile_rows(qd_ref[h])).astype(BF16)
            local = [slice(c * RET_CHUNK, (c + 1) * RET_CHUNK) for c in range(nchunk)]
            kts = [k[r].T for r in local]
            scores = [_dot(qb[r], kt.astype(BF16)) for r, kt in zip(local, kts)]
            kvs = [_dot((kt * kd_ref[h]).astype(BF16), v[r]) for r, kt in zip(local, kts)]
            before = []
            for kv in kvs:
                before.append(states[h].astype(BF16))
                states[h] = states[h] * chunk_dec[h] + kv
            stage1.append((v, qdb, scores, before, local))
        for h, (v, qdb, scores, before, local) in enumerate(stage1):
            lanes = slice(h * LANES, (h + 1) * LANES)
            dec = dec_ref[h]
            for c, r in enumerate(local):
                lhs = jnp.concatenate([(scores[c] * dec).astype(BF16), qdb[r]], axis=1)
                rhs = jnp.concatenate([v[r], before[c]], axis=0)
                o = _dot(lhs, rhs)
                o = o * lax.rsqrt(jnp.mean(o * o, axis=-1, keepdims=True) + RMS_EPS)
                gate = _silu(z_ref[chunks[c], _G0 + h * LANES:_G0 + (h + 1) * LANES])
                cat_ref[chunks[c], lanes] = (gate * o).astype(BF16)

        for gi in range(GM_GROUPS):
            u, gvn = _gmlp_branch(z_ref, slice(None), lng_ref, lnb_ref, gi)
            gvb = gvn.astype(BF16)
            rhs = jnp.concatenate([gvb[c * GM_CHUNK:(c + 1) * GM_CHUNK] for c in range(nchunk)], axis=1)
            mixed = _dot(w_spatial[gi], rhs)
            bias = bs_ref[gi]
            for c in range(nchunk):
                local_rows = slice(c * GM_CHUNK, (c + 1) * GM_CHUNK)
                m = mixed[:, c * GM_CHUNK:(c + 1) * GM_CHUNK] + bias
                cat_ref[chunks[c], 512 + gi * LANES:512 + (gi + 1) * LANES] = (u[local_rows] * m).astype(BF16)

        o_ref[rows, :] = x + _dot(cat_ref[...], wo_ref[...])

    for h in range(RET_HEADS):
        s_ref[h] = states[h]

    @pl.when(si == pl.num_programs(1) - 1)
    def _():
        st_ref[...] = s_ref[...]


def _ab_prompt(x, g, w_in, w_o, w_s, b_s, ln_g, ln_b, batch, seq, cast=()):
    nstep = seq // MIXER_ROWS
    cast_in, cast_out, cast_shapes = _cast_specs(cast, batch * nstep, lambda b, s: b * nstep + s)
    decay, q_dec, k_dec, chunk_dec = _retention_tables(RET_CHUNK)
    rot = _rotary_tables(np.arange(seq))
    lane_b = lambda t: (np if isinstance(t, np.ndarray) else jnp).broadcast_to(t[:, :, None], t.shape + (LANES,))
    row_spec = pl.BlockSpec((MIXER_ROWS, D_MODEL), lambda b, s: (b * nstep + s, 0))
    rot_spec = pl.BlockSpec((MIXER_ROWS, 4 * LANES), lambda b, s: (s, 0))
    cube = (RET_HEADS, RET_CHUNK, LANES)
    out, state, *cast_w = pl.pallas_call(
        functools.partial(_ab_prompt_kernel, chunk_dec=chunk_dec, n_cast=len(cast)),
        grid=(batch, nstep),
        in_specs=[
            row_spec,
            _const_spec((1, D_MODEL)),
            _const_spec((D_MODEL, AB_IN)),
            _const_spec((AB_OUT, D_MODEL)),
            rot_spec,
            _const_spec(cube), _const_spec(cube), _const_spec(cube),
            _const_spec(cube), _const_spec(cube),
            _const_spec((GM_GROUPS, GM_GROUP_DIM)), _const_spec((GM_GROUPS, GM_GROUP_DIM)),
            *cast_in,
        ],
        out_specs=[
            row_spec,
            pl.BlockSpec((None, RET_HEADS, RET_DK, RET_DV), lambda b, s: (b, 0, 0, 0)),
            *cast_out,
        ],
        out_shape=[
            jax.ShapeDtypeStruct((batch * seq, D_MODEL), F32),
            jax.ShapeDtypeStruct((batch, RET_HEADS, RET_DK, RET_DV), F32),
            *cast_shapes,
        ],
        scratch_shapes=(
            [pltpu.VMEM((MIXER_ROWS // MIXER_SPLIT, AB_IN), F32)] * MIXER_SPLIT
            + [pltpu.VMEM((MIXER_ROWS // MIXER_SPLIT, AB_OUT), BF16)] * MIXER_SPLIT
            + [pltpu.VMEM((RET_HEADS, RET_DK, RET_DV), F32)]),
        compiler_params=_params("arbitrary", "arbitrary"),
        name="ab_prompt",
    )(x, g, w_in, w_o, rot, decay, lane_b(q_dec), np.broadcast_to(k_dec[:, None, :], cube),
      w_s, lane_b(b_s), ln_g, ln_b, *(w for w, _ in cast))
    return out, state, cast_w


def _ab_sample_kernel(x_ref, g_ref, win_ref, wo_ref, rot_ref,
                      dec_ref, qd_ref, kd_ref, ws_ref, bs_ref, lng_ref, lnb_ref, causal_ref, st_ref,
                      o_ref, sto_ref, gvn_ref, z_ref, cat_ref, *, chunk_dec, dec_len):
    rows_n = SAMPLE_BB * dec_len
    x = x_ref[...]
    hn = _rms(x, g_ref[...]).astype(BF16)
    z_ref[...] = _dot(hn, win_ref[...])

    cq, sq, ck, sk = (rot_ref[:, i * LANES:(i + 1) * LANES] for i in range(4))
    token = lax.broadcasted_iota(jnp.int32, (rows_n, rows_n), 1)
    for h in range(RET_HEADS):
        lanes = slice(h * LANES, (h + 1) * LANES)
        q = _rotary(z_ref[:, _Q0 + h * LANES:_Q0 + (h + 1) * LANES], cq, sq)
        k = _rotary(z_ref[:, _K0 + h * LANES:_K0 + (h + 1) * LANES], ck, sk)
        v = z_ref[:, _V0 + h * LANES:_V0 + (h + 1) * LANES].astype(BF16)
        scores = _dot_nt(q.astype(BF16), k.astype(BF16)) * dec_ref[h]
        intra = _dot(scores.astype(BF16), v)
        qs = q * qd_ref[h]
        kst = (k * kd_ref[h]).T
        kst_all = jnp.concatenate(
            [jnp.where((token >= b * dec_len) & (token < (b + 1) * dec_len), kst, 0.0).astype(BF16)
             for b in range(SAMPLE_BB)], axis=0)
        kv_all = _dot(kst_all, v)
        cross = []
        for b in range(SAMPLE_BB):
            rows = slice(b * dec_len, (b + 1) * dec_len)
            state = st_ref[b, h]
            cross.append(_dot(qs[rows].astype(BF16), state.astype(BF16)))
            sto_ref[b, h] = state * chunk_dec[h] + kv_all[b * RET_DK:(b + 1) * RET_DK]
        o = intra + jnp.concatenate(cross, axis=0)
        o = o * lax.rsqrt(jnp.mean(o * o, axis=-1, keepdims=True) + RMS_EPS)
        gate = _silu(z_ref[:, _G0 + h * LANES:_G0 + (h + 1) * LANES])
        cat_ref[:, lanes] = (gate * o).astype(BF16)

    causal = causal_ref[...] > 0.0
    for gi in range(GM_GROUPS):
        u, gvn = _gmlp_branch(z_ref, slice(None), lng_ref, lnb_ref, gi)
        gvn_ref[:, gi * LANES:(gi + 1) * LANES] = gvn
        w = jnp.where(causal, ws_ref[gi], 0.0).astype(BF16)
        mixed = _dot(w, gvn.astype(BF16)) + bs_ref[gi]
        cat_ref[:, 512 + gi * LANES:512 + (gi + 1) * LANES] = (u * mixed).astype(BF16)

    o_ref[...] = x + _dot(cat_ref[...], wo_ref[...])


def _ab_sample(x, g, w_in, w_o, w_s, b_s, ln_g, ln_b, state, dec_batch, dec_len):
    rows_n = SAMPLE_BB * dec_len
    assert rows_n == LANES and dec_batch % SAMPLE_BB == 0
    decay, q_dec, k_dec, chunk_dec = _retention_tables(dec_len)
    rot = np.tile(_rotary_tables(PAST_LEN + np.arange(dec_len)), (SAMPLE_BB, 1))
    bd_decay = np.stack([np.kron(np.eye(SAMPLE_BB, dtype=np.float32), d) for d in decay])

    def rows_b(t):
        xp = np if isinstance(t, np.ndarray) else jnp
        return xp.broadcast_to(xp.tile(t, (1, SAMPLE_BB))[:, :, None], (t.shape[0], rows_n, LANES))
    w_tiled = jnp.tile(w_s[:, :dec_len, :dec_len], (1, SAMPLE_BB, SAMPLE_BB))
    causal = jnp.asarray(np.kron(np.eye(SAMPLE_BB), np.tril(np.ones((dec_len, dec_len)))), F32)
    row_spec = pl.BlockSpec((rows_n, D_MODEL), lambda i: (i, 0))
    st_spec = pl.BlockSpec((SAMPLE_BB, RET_HEADS, RET_DK, RET_DV), lambda i: (i, 0, 0, 0))
    cube = (RET_HEADS, rows_n, LANES)
    tab = (rows_n, LANES)
    return pl.pallas_call(
        functools.partial(_ab_sample_kernel, chunk_dec=chunk_dec, dec_len=dec_len),
        grid=(dec_batch // SAMPLE_BB,),
        in_specs=[
            row_spec,
            _const_spec((1, D_MODEL)),
            _const_spec((D_MODEL, AB_IN)),
            _const_spec((AB_OUT, D_MODEL)),
            _const_spec((rows_n, 4 * LANES)),
            _const_spec(cube), _const_spec(cube), _const_spec(cube),
            _const_spec(cube), _const_spec(cube),
            _const_spec((GM_GROUPS, GM_GROUP_DIM)), _const_spec((GM_GROUPS, GM_GROUP_DIM)),
            _const_spec(tab),
            st_spec,
        ],
        out_specs=[
            row_spec,
            st_spec,
            pl.BlockSpec((rows_n, GM_GROUPS * GM_GROUP_DIM), lambda i: (i, 0)),
        ],
        out_shape=[
            jax.ShapeDtypeStruct((dec_batch * dec_len, D_MODEL), F32),
            jax.ShapeDtypeStruct(state.shape, F32),
            jax.ShapeDtypeStruct((dec_batch * dec_len, GM_GROUPS * GM_GROUP_DIM), F32),
        ],
        scratch_shapes=[
            pltpu.VMEM((rows_n, AB_IN), F32),
            pltpu.VMEM((rows_n, AB_OUT), BF16),
        ],
        compiler_params=_params("arbitrary"),
        name="ab_sample",
    )(x, g, w_in, w_o, rot, bd_decay, rows_b(q_dec), rows_b(k_dec),
      w_tiled, rows_b(b_s[:, :dec_len]), ln_g, ln_b, causal, state)


_LOG2E = float(np.log2(np.e))


def _softmax_numerators(s, sink_even, sink_odd):
    s0, s1, s2, s3 = (s[:, i * LANES:(i + 1) * LANES] for i in range(4))
    m_even = jnp.max(jnp.maximum(jnp.maximum(s0, s1), sink_even), axis=-1, keepdims=True)
    m_odd = jnp.max(jnp.maximum(jnp.maximum(s2, s3), sink_odd), axis=-1, keepdims=True)
    p = jnp.concatenate([jnp.exp2(s0 - m_even), jnp.exp2(s1 - m_even),
                         jnp.exp2(s2 - m_odd), jnp.exp2(s3 - m_odd)], axis=1)
    even_lane = lax.broadcasted_iota(jnp.int32, sink_even.shape, 1) < SWA_HD
    sink_term = jnp.where(even_lane, jnp.exp2(sink_even - m_even), jnp.exp2(sink_odd - m_odd))
    return p.astype(BF16), sink_term


def _band_bias(q_rows, first_key_row):
    i = np.arange(q_rows)[:, None]
    j = np.arange(2 * WINDOW)[None, :]
    rel = i + first_key_row - j
    half = np.where((rel >= 0) & (rel <= WINDOW), 0.0, -np.inf).astype(np.float32)
    return np.concatenate([half, half], axis=1)


def _kv_layouts(kcol, vcol):
    even_lane = lax.broadcasted_iota(jnp.int32, kcol.shape, 1) < SWA_HD
    ks = kcol * (SWA_HD ** -0.5 * _LOG2E)
    kr = pltpu.roll(ks, SWA_HD, 1)
    vr = pltpu.roll(vcol, SWA_HD, 1)
    one_even = jnp.where(even_lane, 1.0, 0.0)
    one_odd = 1.0 - one_even
    zero = jnp.zeros_like(kcol)
    out = []
    for kvh in range(SWA_KV_HEADS):
        k_lo, k_hi = (ks, kr) if kvh == 0 else (kr, ks)
        v_lo, v_hi = (vcol, vr) if kvh == 0 else (vr, vcol)
        ktop = jnp.where(even_lane, k_lo, zero).astype(BF16)
        kbot = jnp.where(even_lane, zero, k_hi).astype(BF16)
        vtop = jnp.concatenate([jnp.where(even_lane, v_lo, zero), one_even], axis=1).astype(BF16)
        vbot = jnp.concatenate([jnp.where(even_lane, zero, v_hi), one_odd], axis=1).astype(BF16)
        out.append((ktop, kbot, vtop, vbot))
    return out


def _swa_prompt_kernel(x_ref, xh_ref, g_ref, wqkv_ref, bqkv_ref, wo_ref, bo_ref, sink_ref, bias_ref,
                       o_ref, kc_ref, vc_ref, z_ref, cat_ref):
    si = pl.program_id(1)
    nblk = MIXER_ROWS // WINDOW
    npair = SWA_HEADS // SWA_KV_HEADS // 2
    sub_rows = MIXER_ROWS // MIXER_SPLIT

    x = x_ref[...]
    hn = _rms(x, g_ref[...]).astype(BF16)
    for sub in range(MIXER_SPLIT):
        rows = slice(sub * sub_rows, (sub + 1) * sub_rows)
        z_ref[rows, :] = _dot(hn[rows], wqkv_ref[...]) + bqkv_ref[...]
    hh = _rms(xh_ref[...], g_ref[...]).astype(BF16)
    zh = _dot(hh, wqkv_ref[:, _SK0:]) + bqkv_ref[:, _SK0:]

    kcol = z_ref[:, _SK0:_SK0 + LANES]
    vcol = z_ref[:, _SV0:_SV0 + LANES]
    kc_ref[...] = kcol[MIXER_ROWS - WINDOW:].T
    vc_ref[...] = vcol[MIXER_ROWS - WINDOW:].T
    layouts = _kv_layouts(jnp.concatenate([zh[:, :LANES], kcol], axis=0),
                          jnp.concatenate([zh[:, LANES:], vcol], axis=0))

    sink_rows = lambda tab: jnp.concatenate(
        [tab[p * 8:(p + 1) * 8] for p in range(npair) for _ in range(WINDOW // 8)], axis=0)
    sinks = [[sink_rows(sink_ref[kvh, par]) for par in range(2)] for kvh in range(SWA_KV_HEADS)]

    for r in range(nblk):
        rows = slice(r * WINDOW, (r + 1) * WINDOW)
        keys = slice(r * WINDOW, (r + 2) * WINDOW)
        tile = bias_ref[jnp.where(si > 0, 0, 1)] if r == 0 else bias_ref[0]
        bias = jnp.concatenate([tile] * npair, axis=0)
        for kvh, (ktop, kbot, vtop, vbot) in enumerate(layouts):
            c0 = kvh * npair * LANES
            lhs = jnp.concatenate(
                [z_ref[rows, c0 + p * LANES:c0 + (p + 1) * LANES] for p in range(npair)], axis=0).astype(BF16)
            s = _dot_nt(lhs, jnp.concatenate([ktop[keys], kbot[keys]], axis=0)) + bias
            p, sink_term = _softmax_numerators(s, sinks[kvh][0], sinks[kvh][1])
            out = _dot(p, jnp.concatenate([vtop[keys], vbot[keys]], axis=0))
            o = out[:, :LANES] / (out[:, LANES:] + sink_term)
            for p_i in range(npair):
                cat_ref[rows, c0 + p_i * LANES:c0 + (p_i + 1) * LANES] = (
                    o[p_i * WINDOW:(p_i + 1) * WINDOW].astype(BF16))

    o_ref[...] = x + _dot(cat_ref[...], wo_ref[...]) + bo_ref[...]


def _sink_table(sinks, rows_per_pair):
    npair = SWA_HEADS // SWA_KV_HEADS // 2
    s = (sinks.astype(F32) * _LOG2E).reshape(SWA_KV_HEADS, npair, 2).transpose(0, 2, 1)[:, :, :, None, None]
    s = jnp.broadcast_to(s, (SWA_KV_HEADS, 2, npair, rows_per_pair, LANES))
    return s.reshape(SWA_KV_HEADS, 2, npair * rows_per_pair, LANES)


def _swa_prompt(x, g, w_qkv, b_qkv, sinks, w_o, b_o, batch, seq):
    nstep = seq // MIXER_ROWS
    npair = SWA_HEADS // SWA_KV_HEADS // 2
    row_spec = pl.BlockSpec((MIXER_ROWS, D_MODEL), lambda b, s: (b * nstep + s, 0))
    nblk = MIXER_ROWS // WINDOW
    halo_spec = pl.BlockSpec((WINDOW, D_MODEL), lambda b, s: (jnp.maximum((b * nstep + s) * nblk - 1, 0), 0))
    cache_spec = pl.BlockSpec((None, WINDOW, LANES), lambda b, s: (b, 0, 0))
    band = _band_bias(WINDOW, WINDOW)
    no_prev = np.where(np.arange(4 * WINDOW)[None, :] % (2 * WINDOW) >= WINDOW, band, -np.inf).astype(np.float32)
    bias = jnp.asarray(np.stack([band, no_prev]))
    return pl.pallas_call(
        _swa_prompt_kernel,
        grid=(batch, nstep),
        in_specs=[
            row_spec,
            halo_spec,
            _const_spec((1, D_MODEL)),
            _const_spec((D_MODEL, SWA_IN)),
            _const_spec((1, SWA_IN)),
            _const_spec((SWA_OUT, D_MODEL)),
            _const_spec((1, D_MODEL)),
            _const_spec((SWA_KV_HEADS, 2, npair * 8, LANES)),
            _const_spec((2, WINDOW, 4 * WINDOW)),
        ],
        out_specs=[row_spec, cache_spec, cache_spec],
        out_shape=[
            jax.ShapeDtypeStruct((batch * seq, D_MODEL), F32),
            jax.ShapeDtypeStruct((batch, WINDOW, LANES), F32),
            jax.ShapeDtypeStruct((batch, WINDOW, LANES), F32),
        ],
        scratch_shapes=[
            pltpu.VMEM((MIXER_ROWS, SWA_IN), F32),
            pltpu.VMEM((MIXER_ROWS, SWA_OUT), BF16),
        ],
        compiler_params=_params("arbitrary", "arbitrary"),
        name="swa_prompt",
    )(x, x, g, w_qkv, b_qkv, w_o, b_o, _sink_table(sinks, 8), bias)


def _swa_sample_kernel(x_ref, g_ref, wqkv_ref, bqkv_ref, wo_ref, bo_ref, sink_ref, bias_ref, ck_ref, cv_ref,
                       o_ref, nk_ref, nv_ref, z_ref, cat_ref, *, dec_len):
    ncol = SWA_HEADS // 2
    grp = ncol * dec_len
    x = x_ref[...]
    hn = _rms(x, g_ref[...]).astype(BF16)
    z_ref[...] = _dot(hn, wqkv_ref[...]) + bqkv_ref[...]

    bias, sink = bias_ref[...], sink_ref[...]
    lane_lo = lax.broadcasted_iota(jnp.int32, (grp, LANES), 1) < SWA_HD
    kv0 = lax.broadcasted_iota(jnp.int32, (grp, LANES), 0) < grp // 2
    own_half = jnp.where(kv0, 0, 1) == jnp.where(lane_lo, 0, 1)
    pad = jnp.zeros((WINDOW - dec_len, LANES), F32)
    elems = [slice(b * dec_len, (b + 1) * dec_len) for b in range(SAMPLE_BB)]
    values, scores = [], []
    for b, rows in enumerate(elems):
        knew = z_ref[rows, _SK0:_SK0 + LANES]
        vnew = z_ref[rows, _SV0:_SV0 + LANES]
        k_t, v_t = ck_ref[b], cv_ref[b]
        nk_ref[b, 0:WINDOW - dec_len, :] = k_t.T[dec_len:]
        nk_ref[b, WINDOW - dec_len:, :] = knew
        nv_ref[b, 0:WINDOW - dec_len, :] = v_t.T[dec_len:]
        nv_ref[b, WINDOW - dec_len:, :] = vnew
        knew_pad = jnp.concatenate([knew, pad], axis=0).astype(BF16)
        values.append((v_t.astype(BF16), jnp.concatenate([vnew, pad], axis=0).astype(BF16)))

        q = jnp.concatenate([z_ref[rows, j * LANES:(j + 1) * LANES] for j in range(ncol)], axis=0)
        q = q * (SWA_HD ** -0.5 * _LOG2E)
        q_swapped = pltpu.roll(q, SWA_HD, 1)
        q_even = jnp.where(own_half, jnp.where(kv0, q, q_swapped), 0.0)
        q_odd = jnp.where(own_half, jnp.where(kv0, q_swapped, q), 0.0)
        lhs = jnp.concatenate([q_even, q_odd], axis=0).astype(BF16)
        scores.append(jnp.concatenate([_dot(lhs, k_t.astype(BF16)), _dot_nt(lhs, knew_pad)], axis=1) + bias)

    probs, dens = [], []
    for s in scores:
        m = jnp.max(jnp.maximum(jnp.maximum(s[:, :LANES], s[:, LANES:]), sink), axis=-1, keepdims=True)
        p = jnp.exp2(s - m)
        dens.append(jnp.sum(p, axis=-1, keepdims=True) + jnp.exp2(sink - m))
        probs.append(p.astype(BF16))

    outs = [_dot_nt(p[:, :WINDOW], v_old_t) + _dot(p[:, WINDOW:], v_new) for p, (v_old_t, v_new) in zip(probs, values)]

    for rows, out, den in zip(elems, outs, dens):
        o = out / den
        o_swapped = pltpu.roll(o, SWA_HD, 1)
        cols = jnp.where(lane_lo,
                         jnp.where(kv0, o[:grp], o_swapped[:grp]),
                         jnp.where(kv0, o_swapped[grp:], o[grp:]))
        for j in range(ncol):
            cat_ref[rows, j * LANES:(j + 1) * LANES] = cols[j * dec_len:(j + 1) * dec_len]

    o_ref[...] = x + _dot(cat_ref[...].astype(BF16), wo_ref[...]) + bo_ref[...]


def _swa_sample(x, g, w_qkv, b_qkv, sinks, w_o, b_o, cache_k, cache_v, dec_batch, dec_len):
    rows_n = SAMPLE_BB * dec_len
    ncol = SWA_HEADS // 2
    score_rows = 2 * ncol * dec_len
    sink = jnp.broadcast_to((sinks.astype(F32) * _LOG2E).reshape(ncol, 2).T[:, :, None, None],
                            (2, ncol, dec_len, LANES)).reshape(score_rows, LANES)
    bias = jnp.asarray(np.tile(_band_bias(dec_len, WINDOW)[:, :2 * WINDOW], (2 * ncol, 1)))
    row_spec = pl.BlockSpec((rows_n, D_MODEL), lambda i: (i, 0))
    cache_spec = pl.BlockSpec((SAMPLE_BB, WINDOW, LANES), lambda i: (i, 0, 0))
    return pl.pallas_call(
        functools.partial(_swa_sample_kernel, dec_len=dec_len),
        grid=(dec_batch // SAMPLE_BB,),
        in_specs=[
            row_spec,
            _const_spec((1, D_MODEL)),
            _const_spec((D_MODEL, SWA_IN)),
            _const_spec((1, SWA_IN)),
            _const_spec((SWA_OUT, D_MODEL)),
            _const_spec((1, D_MODEL)),
            _const_spec((score_rows, LANES)),
            _const_spec((score_rows, 2 * WINDOW)),
            cache_spec, cache_spec,
        ],
        out_specs=[row_spec, cache_spec, cache_spec],
        out_shape=[
            jax.ShapeDtypeStruct((dec_batch * dec_len, D_MODEL), F32),
            jax.ShapeDtypeStruct((dec_batch, WINDOW, LANES), F32),
            jax.ShapeDtypeStruct((dec_batch, WINDOW, LANES), F32),
        ],
        scratch_shapes=[
            pltpu.VMEM((rows_n, SWA_IN), F32),
            pltpu.VMEM((rows_n, SWA_OUT), F32),
        ],
        compiler_params=_params("arbitrary"),
        name="swa_sample",
    )(x, g, w_qkv, b_qkv, w_o, b_o, sink, bias, cache_k, cache_v)


def kernel(x_prompt, x_sample, state_ret, cache_swa_k, cache_swa_v, norm_mix, norm_mlp, norm_final, ab_w_in, ab_w_s, ab_b_s, ab_ln_g, ab_ln_b, ab_w_o, swa_w_qkv, swa_b_qkv, swa_sinks, swa_w_o, swa_b_o, mlp_w_up, mlp_w_down):
    batch, seq, _ = x_prompt.shape
    dec_batch, dec_len, _ = x_sample.shape
    assert seq % MIXER_ROWS == 0 and seq % RET_CHUNK == 0
    kv_lanes = SWA_KV_HEADS * SWA_HD

    hp = x_prompt.reshape(batch * seq, D_MODEL)
    hs = x_sample.reshape(dec_batch * dec_len, D_MODEL)
    row = lambda v: v.reshape(1, -1)
    gfin = row(norm_final)

    w_in, w_o = ab_w_in[0].astype(BF16), ab_w_o[0].astype(BF16)
    hp, ret_p, (w_up, w_dn) = _ab_prompt(hp, row(norm_mix[0]), w_in, w_o, ab_w_s[0], ab_b_s[0], ab_ln_g[0],
                                         ab_ln_b[0], batch, seq, cast=[(mlp_w_up, 0), (mlp_w_down, 0)])
    hs, ret_s, gm_s = _ab_sample(hs, row(norm_mix[0]), w_in, w_o, ab_w_s[0], ab_b_s[0], ab_ln_g[0], ab_ln_b[0],
                                 state_ret[0], dec_batch, dec_len)
    hp, (w_qkv, w_so, w_up1, w_dn1) = _mlp(hp, row(norm_mlp[0]), w_up, w_dn, gfin, final_norm=False,
                                           name="mlp0_prompt",
                                           cast=[(swa_w_qkv, 0), (swa_w_o, 0), (mlp_w_up, 1), (mlp_w_down, 1)])
    hs, _ = _mlp(hs, row(norm_mlp[0]), w_up, w_dn, gfin, final_norm=False, name="mlp0_sample")

    cache_t = lambda c: jnp.transpose(c[0], (0, 2, 3, 1)).reshape(dec_batch, kv_lanes, WINDOW)
    hp, kp, vp = _swa_prompt(hp, row(norm_mix[1]), w_qkv, row(swa_b_qkv[0]), swa_sinks[0], w_so, row(swa_b_o[0]),
                             batch, seq)
    hs, ks, vs = _swa_sample(hs, row(norm_mix[1]), w_qkv, row(swa_b_qkv[0]), swa_sinks[0], w_so, row(swa_b_o[0]),
                             cache_t(cache_swa_k), cache_t(cache_swa_v), dec_batch, dec_len)
    hp, _ = _mlp(hp, row(norm_mlp[1]), w_up1, w_dn1, gfin, final_norm=True, name="mlp1_prompt")
    hs, _ = _mlp(hs, row(norm_mlp[1]), w_up1, w_dn1, gfin, final_norm=True, name="mlp1_sample")

    cache_shape = lambda n: (1, n, WINDOW, SWA_KV_HEADS, SWA_HD)
    prompt_cache = lambda c: jnp.transpose(c.reshape(batch, SWA_KV_HEADS, SWA_HD, WINDOW), (0, 3, 1, 2))[None]
    return (hp.reshape(batch, seq, D_MODEL),
            hs.reshape(dec_batch, dec_len, D_MODEL),
            ret_p[None],
            ret_s[None],
            gm_s.reshape(1, dec_batch, dec_len, GM_GROUPS * GM_GROUP_DIM),
            prompt_cache(kp),
            prompt_cache(vp),
            ks.reshape(cache_shape(dec_batch)),
            vs.reshape(cache_shape(dec_batch)))
```
